```python
import math
import jax, jax.numpy as jnp
from jax import lax
import numpy as np

D_MODEL = 1024
BATCH = 8
SEQ = 8192
DEPTH = 1

D_MIX = 2 * D_MODEL
HG_WIDTH = D_MODEL
HG_HEAD_DIM = 128
HG_HEADS = HG_WIDTH // HG_HEAD_DIM
HG_CHUNK = 64
SSD_WIDTH = D_MIX - HG_WIDTH
SSD_HEAD_DIM = 64
SSD_HEADS = SSD_WIDTH // SSD_HEAD_DIM
SSD_GROUPS = 2
SSD_STATE = 128
SSD_CONV = 4
SSD_CHUNK = 64
SSD_CONV_DIM = SSD_WIDTH + 2 * SSD_GROUPS * SSD_STATE
IN_COLS = 4 * HG_WIDTH + SSD_WIDTH + SSD_CONV_DIM + SSD_HEADS
IN_SPLITS = (HG_WIDTH, 2 * HG_WIDTH, 3 * HG_WIDTH, 4 * HG_WIDTH,
             4 * HG_WIDTH + SSD_WIDTH, 4 * HG_WIDTH + SSD_WIDTH + SSD_CONV_DIM)
PEER_HEADS = 8
PEER_N_KEYS = 128
PEER_N_EXPERTS = PEER_N_KEYS * PEER_N_KEYS
PEER_KEY_DIM = 128
PEER_TOPK = 16
PEER_BLOCK = 128
EPS = 1e-6

kernel_name = "hymba_hgrn2_ssd_peer_block"


def rms_norm(x, w):
    xf = x.astype(jnp.float32)
    y = xf * lax.rsqrt(jnp.mean(xf * xf, axis=-1, keepdims=True) + EPS)
    return (y * w.astype(jnp.float32)).astype(x.dtype)


def causal_mask(n):
    return jnp.tril(jnp.ones((n, n), dtype=bool))


def chunk_scan(decay, chunk_states):
    def step(state, inp):
        d, s_new = inp
        return d * state + s_new, state
    init = jnp.zeros_like(chunk_states[:, 0])
    _, starts = lax.scan(step, init, (jnp.moveaxis(decay, 1, 0), jnp.moveaxis(chunk_states, 1, 0)))
    return jnp.moveaxis(starts, 0, 1)


def hgrn2_mixer(q, f_raw, i_in, g_out, lb, norm_w):
    f32 = jnp.float32
    B, S, _ = q.shape
    C = HG_CHUNK
    nc = S // C
    f = lb.astype(f32) + (1.0 - lb.astype(f32)) * jax.nn.sigmoid(f_raw.astype(f32))
    k = 1.0 - f
    logf = jnp.log(f)

    def chunks(t):
        return t.astype(f32).reshape(B, nc, C, HG_HEADS, HG_HEAD_DIM)

    qc, kc, vc, gc = chunks(q), chunks(k), chunks(i_in), chunks(logf)
    G = jnp.cumsum(gc, axis=2)
    G_last = G[:, :, -1]
    q_dec = qc * jnp.exp(G)
    k_dec = kc * jnp.exp(-G)
    scores = jnp.einsum('bcthk,bcshk->bchts', q_dec, k_dec)
    scores = jnp.where(causal_mask(C), scores, 0.0)
    o_intra = jnp.einsum('bchts,bcshv->bcthv', scores, vc)
    k_to_end = kc * jnp.exp(G_last[:, :, None] - G)
    chunk_states = jnp.einsum('bcshk,bcshv->bchkv', k_to_end, vc)
    S_starts = chunk_scan(jnp.exp(G_last)[..., None], chunk_states)
    o_inter = jnp.einsum('bcthk,bchkv->bcthv', q_dec, S_starts)
    o = (o_intra + o_inter).reshape(B, S, HG_HEADS, HG_HEAD_DIM)
    o = rms_norm(o, norm_w) * jax.nn.silu(g_out.astype(f32).reshape(B, S, HG_HEADS, HG_HEAD_DIM))
    return o.reshape(B, S, HG_WIDTH)


def ssd_mixer(z, xbc, dt_raw, conv_w, conv_b, dt_bias, A_log, D_skip, norm_w):
    f32 = jnp.float32
    B, S, _ = z.shape
    C = SSD_CHUNK
    nc = S // C
    G, Hg, P, N = SSD_GROUPS, SSD_HEADS // SSD_GROUPS, SSD_HEAD_DIM, SSD_STATE
    xbc = lax.conv_general_dilated(
        xbc.astype(f32), conv_w.astype(f32)[:, None, :], window_strides=(1,),
        padding=[(SSD_CONV - 1, 0)], dimension_numbers=('NWC', 'WIO', 'NWC'),
        feature_group_count=SSD_CONV_DIM)
    xbc = jax.nn.silu(xbc + conv_b.astype(f32))
    xs, Bm, Cm = jnp.split(xbc, [SSD_WIDTH, SSD_WIDTH + G * N], axis=-1)
    xs = xs.reshape(B, nc, C, G, Hg, P)
    Bm = Bm.reshape(B, nc, C, G, N)
    Cm = Cm.reshape(B, nc, C, G, N)
    dt = jax.nn.softplus(dt_raw.astype(f32) + dt_bias.astype(f32))
    A = -jnp.exp(A_log.astype(f32))
    dt_c = dt.reshape(B, nc, C, G, Hg)
    xdt = xs * dt_c[..., None]
    cumA = jnp.cumsum(jnp.moveaxis(dt_c * A.reshape(G, Hg), 2, -1), axis=-1)
    diff = cumA[..., :, None] - cumA[..., None, :]
    L = jnp.exp(jnp.where(causal_mask(C), diff, -jnp.inf))
    CB = jnp.einsum('bctgn,bcsgn->bcgts', Cm, Bm)
    y_diag = jnp.einsum('bcghts,bcsghp->bctghp', CB[:, :, :, None] * L, xdt)
    decay_to_end = jnp.exp(cumA[..., -1:] - cumA)
    chunk_states = jnp.einsum('bcsgn,bcghs,bcsghp->bcghpn', Bm, decay_to_end, xdt)
    S_starts = chunk_scan(jnp.exp(cumA[..., -1])[..., None, None], chunk_states)
    y_off = jnp.einsum('bctgn,bcghpn,bcght->bctghp', Cm, S_starts, jnp.exp(cumA))
    y = y_diag + y_off + D_skip.astype(f32).reshape(G, Hg, 1) * xs
    y = y.reshape(B, S, SSD_WIDTH) * jax.nn.silu(z.astype(f32))
    y = rms_norm(y.reshape(B, S, G, SSD_WIDTH // G), norm_w.reshape(G, SSD_WIDTH // G))
    return y.reshape(B, S, SSD_WIDTH)


def peer_ffn(h, w_q, keys, u, v):
    f32 = jnp.float32
    B, S, D = h.shape
    K = PEER_TOPK
    blocks = h.reshape((B * S) // PEER_BLOCK, PEER_BLOCK, D)

    def block_fn(xb):
        q = (xb @ w_q).reshape(PEER_BLOCK, PEER_HEADS, 2, PEER_KEY_DIM)
        sc = jnp.einsum('thpd,hpkd->thpk', q, keys).astype(f32)
        top_v, top_i = lax.top_k(sc, K)
        cand = (top_v[:, :, 0, :, None] + top_v[:, :, 1, None, :]).reshape(PEER_BLOCK, PEER_HEADS, K * K)
        best_v, best_i = lax.top_k(cand, K)
        i1 = jnp.take_along_axis(top_i[:, :, 0], best_i // K, axis=-1)
        i2 = jnp.take_along_axis(top_i[:, :, 1], best_i % K, axis=-1)
        experts = (i1 * PEER_N_KEYS + i2).reshape(PEER_BLOCK, PEER_HEADS * K)
        gate = jax.nn.softmax(best_v, axis=-1).reshape(PEER_BLOCK, PEER_HEADS * K)
        u_sel = u[experts]
        v_sel = v[experts]
        act = jax.nn.gelu(jnp.einsum('ted,td->te', u_sel, xb).astype(f32), approximate=False) * gate
        return jnp.einsum('te,ted->td', act.astype(v_sel.dtype), v_sel)

    out = lax.map(block_fn, blocks)
    return out.reshape(B, S, D)


def setup_inputs(seed: int = 0) -> dict:
    key = jax.random.key(seed)
    ks = jax.random.split(key, 18)
    f32 = jnp.float32
    Ld = DEPTH

    def nrm(k, shape, scale):
        return jax.random.normal(k, shape, f32) * scale

    dt0 = jnp.exp(jax.random.uniform(ks[7], (Ld, SSD_HEADS), f32, math.log(1e-3), math.log(1e-1)))
    return {
        "x": nrm(ks[0], (BATCH, SEQ, D_MODEL), 1.0),
        "norm_mix_w": 1.0 + nrm(ks[1], (Ld, D_MODEL), 0.02),
        "w_in": nrm(ks[2], (Ld, D_MODEL, IN_COLS), D_MODEL ** -0.5),
        "hg_lb_logits": nrm(ks[3], (Ld + 1, HG_WIDTH), 0.1),
        "hg_norm_w": 1.0 + nrm(ks[4], (Ld, HG_HEAD_DIM), 0.02),
        "ssd_conv_w": nrm(ks[5], (Ld, SSD_CONV, SSD_CONV_DIM), SSD_CONV ** -0.5),
        "ssd_conv_b": nrm(ks[6], (Ld, SSD_CONV_DIM), 0.02),
        "ssd_dt_bias": dt0 + jnp.log(-jnp.expm1(-dt0)),
        "ssd_A_log": jnp.log(jax.random.uniform(ks[8], (Ld, SSD_HEADS), f32, 1.0, 16.0)),
        "ssd_D": 1.0 + nrm(ks[9], (Ld, SSD_HEADS), 0.1),
        "ssd_norm_w": 1.0 + nrm(ks[10], (Ld, SSD_WIDTH), 0.02),
        "w_out": nrm(ks[11], (Ld, D_MIX, D_MODEL), D_MIX ** -0.5),
        "norm_ffn_w": 1.0 + nrm(ks[12], (Ld, D_MODEL), 0.02),
        "peer_w_q": nrm(ks[13], (Ld, D_MODEL, PEER_HEADS * 2 * PEER_KEY_DIM), D_MODEL ** -0.5),
        "peer_keys": nrm(ks[14], (Ld, PEER_HEADS, 2, PEER_N_KEYS, PEER_KEY_DIM), PEER_KEY_DIM ** -0.5),
        "peer_u": nrm(ks[15], (Ld, PEER_N_EXPERTS, D_MODEL), D_MODEL ** -0.5),
        "peer_v": nrm(ks[16], (Ld, PEER_N_EXPERTS, D_MODEL), PEER_HEADS ** -0.5),
        "final_norm_w": 1.0 + nrm(ks[17], (D_MODEL,), 0.02),
    }


def reference(x, norm_mix_w, w_in, hg_lb_logits, hg_norm_w, ssd_conv_w, ssd_conv_b,
              ssd_dt_bias, ssd_A_log, ssd_D, ssd_norm_w, w_out, norm_ffn_w,
              peer_w_q, peer_keys, peer_u, peer_v, final_norm_w):
    lbs = jnp.cumsum(jax.nn.softmax(hg_lb_logits.astype(jnp.float32), axis=0), axis=0)
    for l in range(DEPTH):
        h = rms_norm(x, norm_mix_w[l])
        proj = h @ w_in[l]
        q, f_raw, i_in, g_out, z, xbc, dt_raw = jnp.split(proj, IN_SPLITS, axis=-1)
        o_hg = hgrn2_mixer(q, f_raw, i_in, g_out, lbs[l], hg_norm_w[l])
        o_ssd = ssd_mixer(z, xbc, dt_raw, ssd_conv_w[l], ssd_conv_b[l], ssd_dt_bias[l],
                          ssd_A_log[l], ssd_D[l], ssd_norm_w[l])
        mixed = jnp.concatenate([o_hg, o_ssd], axis=-1).astype(x.dtype)
        x = x + mixed @ w_out[l]
        x = x + peer_ffn(rms_norm(x, norm_ffn_w[l]), peer_w_q[l], peer_keys[l], peer_u[l], peer_v[l])
    return rms_norm(x, final_norm_w)
```

```python
import functools
import math

import jax
import jax.numpy as jnp
from jax import lax
from jax.experimental import pallas as pl
from jax.experimental.pallas import tpu as pltpu

F32 = jnp.float32
BF16 = jnp.bfloat16
EPS = 1e-6
HIGHEST = lax.Precision.HIGHEST

D_MODEL = 1024
HG_HEADS = 8
HG_HEAD_DIM = 128
CHUNK = 64
SSD_HEADS = 16
SSD_HEAD_DIM = 64
SSD_GROUPS = 2
SSD_STATE = 128
SSD_CONV = 4
SSD_GROUP_WIDTH = 512
PEER_HEADS = 8
PEER_KEYS = 128
PEER_TOPK = 16
PROJ_MAIN = 6656
LANES = 128

VMEM_LIMIT_BYTES = 56 * 1024 * 1024


def _dot(a, b):
    return jnp.dot(a, b, preferred_element_type=F32)


def _dot_nt(a, b):
    return lax.dot_general(a, b, (((1,), (1,)), ((), ())), preferred_element_type=F32)


def _dot_tn(a, b):
    return lax.dot_general(a, b, (((0,), (0,)), ((), ())), preferred_element_type=F32)


def _dot_exact(a, b):
    return jnp.dot(a, b, preferred_element_type=F32, precision=HIGHEST)


def _sigmoid(x):
    return 1.0 / (1.0 + jnp.exp(-x))


def _silu(x):
    return x * _sigmoid(x)


def _softplus(x):
    return jnp.maximum(x, 0.0) + jnp.log(1.0 + jnp.exp(-jnp.abs(x)))


def _gelu(x):
    return 0.5 * x * (1.0 + lax.erf(x * (1.0 / math.sqrt(2.0))))


def _tri(n, m=None):
    m = n if m is None else m
    r = lax.broadcasted_iota(jnp.int32, (n, m), 0)
    c = lax.broadcasted_iota(jnp.int32, (n, m), 1)
    return r >= (c % n)


def _inproj_body(x_ref, nw_ref, w_ref, wdt_ref, o_ref, dt_ref, hn_ref):
    @pl.when(pl.program_id(1) == 0)
    def _():
        x = x_ref[...]
        ms = jnp.mean(x * x, axis=-1, keepdims=True)
        hn = (x * lax.rsqrt(ms + EPS) * nw_ref[...]).astype(BF16)
        hn_ref[...] = hn
        dt_ref[...] = _dot(hn, wdt_ref[...])

    o_ref[...] = _dot(hn_ref[...], w_ref[...]).astype(o_ref.dtype)


def _inproj(x2d, norm_w, w_main, w_dt, tm, tn):
    n = x2d.shape[0]
    return pl.pallas_call(
        _inproj_body,
        grid=(n // tm, PROJ_MAIN // tn),
        in_specs=[
            pl.BlockSpec((tm, D_MODEL), lambda i, j: (i, 0)),
            pl.BlockSpec((1, D_MODEL), lambda i, j: (0, 0)),
            pl.BlockSpec((D_MODEL, tn), lambda i, j: (0, j)),
            pl.BlockSpec((D_MODEL, LANES), lambda i, j: (0, 0)),
        ],
        out_specs=[
            pl.BlockSpec((tm, tn), lambda i, j: (i, j)),
            pl.BlockSpec((tm, LANES), lambda i, j: (i, 0)),
        ],
        out_shape=[
            jax.ShapeDtypeStruct((n, PROJ_MAIN), F32),
            jax.ShapeDtypeStruct((n, LANES), F32),
        ],
        scratch_shapes=[pltpu.VMEM((tm, D_MODEL), BF16)],
        compiler_params=pltpu.CompilerParams(
            dimension_semantics=("arbitrary", "arbitrary"),
            vmem_limit_bytes=VMEM_LIMIT_BYTES),
        name="inproj",
    )(x2d, norm_w, w_main, w_dt)


def _hgrn2_body(q_ref, f_ref, i_ref, g_ref, lbl_ref, nw_ref, o_ref, st_ref, *, n_chunks):
    @pl.when(pl.program_id(2) == 0)
    def _():
        st_ref[...] = jnp.zeros_like(st_ref)

    lbl = lbl_ref[...]
    e = jnp.exp(lbl - jnp.max(lbl, axis=0, keepdims=True))
    lb = e[0:1, :] / jnp.sum(e, axis=0, keepdims=True)
    tri = _tri(CHUNK)
    tri_f = tri.astype(F32)
    nw = nw_ref[...]

    def chunk(c, carry):
        sl = pl.ds(pl.multiple_of(c * CHUNK, CHUNK), CHUNK)
        f = lb + (1.0 - lb) * _sigmoid(f_ref[sl, :])
        k = 1.0 - f
        g_cum = _dot_exact(tri_f, jnp.log(f))
        g_last = g_cum[CHUNK - 1:CHUNK, :]
        q_dec = (q_ref[sl, :] * jnp.exp(g_cum)).astype(BF16)
        k_dec = (k * jnp.exp(-g_cum)).astype(BF16)
        v = i_ref[sl, :].astype(BF16)
        scores = jnp.where(tri, _dot_nt(q_dec, k_dec), 0.0)
        st = st_ref[...]
        o = _dot(scores.astype(BF16), v) + _dot_nt(q_dec, st.astype(BF16))
        k_end = (k * jnp.exp(g_last - g_cum)).astype(BF16)
        st_ref[...] = st * jnp.exp(g_last) + _dot_tn(v, k_end)
        ms = jnp.mean(o * o, axis=-1, keepdims=True)
        o_ref[sl, :] = o * lax.rsqrt(ms + EPS) * nw * _silu(g_ref[sl, :])
        return carry

    lax.fori_loop(0, n_chunks, chunk, 0)


def _hgrn2(proj, lb_logits, norm_w, batch, seq, ts):
    n = proj.shape[0]
    nt = seq // ts
    hd = HG_HEAD_DIM

    def col(k):
        return pl.BlockSpec((ts, hd), lambda b, h, t: (b * nt + t, k * HG_HEADS + h))

    return pl.pallas_call(
        functools.partial(_hgrn2_body, n_chunks=ts // CHUNK),
        grid=(batch, HG_HEADS, nt),
        in_specs=[
            col(0), col(1), col(2), col(3),
            pl.BlockSpec((2, hd), lambda b, h, t: (0, h)),
            pl.BlockSpec((1, hd), lambda b, h, t: (0, 0)),
        ],
        out_specs=pl.BlockSpec((ts, hd), lambda b, h, t: (b * nt + t, h)),
        out_shape=jax.ShapeDtypeStruct((n, HG_HEADS * hd), F32),
        scratch_shapes=[pltpu.VMEM((hd, hd), F32)],
        compiler_params=pltpu.CompilerParams(
            dimension_semantics=("arbitrary", "arbitrary", "arbitrary"),
            vmem_limit_bytes=VMEM_LIMIT_BYTES),
        name="hgrn2",
    )(proj, proj, proj, proj, lb_logits, norm_w)


def _ssd_body(z_ref, xs_ref, bc_ref, dt_ref, cwx_ref, cwb_ref, cbx_ref, cbb_ref,
              dtb_ref, alog_ref, dx_ref, nw_ref, o_ref,
              xpad_ref, bpad_ref, xc_ref, bcc_ref, st_ref, *, ts):
    width = SSD_GROUPS * SSD_GROUP_WIDTH
    bcw = 2 * SSD_GROUPS * SSD_STATE

    @pl.when(pl.program_id(1) == 0)
    def _():
        st_ref[...] = jnp.zeros_like(st_ref)
        xpad_ref[0:8, :] = jnp.zeros((8, width), F32)
        bpad_ref[0:8, :] = jnp.zeros((8, bcw), F32)

    xpad_ref[8:, :] = xs_ref[...]
    bpad_ref[8:, :] = bc_ref[...]
    accx = cbx_ref[...]
    accb = cbb_ref[...]
    for j in range(SSD_CONV):
        lo = 8 - (SSD_CONV - 1) + j
        accx = accx + xpad_ref[lo:lo + ts, :] * cwx_ref[j:j + 1, :]
        accb = accb + bpad_ref[lo:lo + ts, :] * cwb_ref[j:j + 1, :]
    xc_ref[...] = _silu(accx)
    bcc_ref[...] = _silu(accb)
    xpad_ref[0:8, :] = xs_ref[ts - 8:ts, :]
    bpad_ref[0:8, :] = bc_ref[ts - 8:ts, :]

    a_row = -jnp.exp(alog_ref[...])
    dtb = dtb_ref[...]
    tri_f = _tri(CHUNK).astype(F32)
    tri2 = _tri(CHUNK, 2 * CHUNK)
    lane = lax.broadcasted_iota(jnp.int32, (1, LANES), 1)
    left = lane < SSD_HEAD_DIM
    hrow = lax.broadcasted_iota(jnp.int32, (LANES, width), 0)
    hcol = lax.broadcasted_iota(jnp.int32, (LANES, width), 1) // SSD_HEAD_DIM
    expand = (hrow == hcol).astype(F32)

    def chunk(c, carry):
        sl = pl.ds(pl.multiple_of(c * CHUNK, CHUNK), CHUNK)
        dt = _softplus(dt_ref[sl, :] + dtb)
        cum = _dot_exact(tri_f, dt * a_row)
        cum_x = _dot_exact(cum, expand)
        dt_x = _dot_exact(dt, expand)
        cum2 = jnp.concatenate([cum, cum], axis=0)
        cum2_t = cum2.T
        xc = xc_ref[sl, :]
        xdt = xc * dt_x
        last_x = cum_x[CHUNK - 1:CHUNK, :]
        xd_end = (xdt * jnp.exp(last_x - cum_x)).astype(BF16)
        ecum_x = jnp.exp(cum_x)
        dec_x = jnp.exp(last_x)
        xdt_b = xdt.astype(BF16)
        ys = []
        for g in range(SSD_GROUPS):
            gs = slice(g * SSD_GROUP_WIDTH, (g + 1) * SSD_GROUP_WIDTH)
            bm = bcc_ref[sl, g * SSD_STATE:(g + 1) * SSD_STATE].astype(BF16)
            cm = bcc_ref[sl, (SSD_GROUPS + g) * SSD_STATE:
                         (SSD_GROUPS + g + 1) * SSD_STATE].astype(BF16)
            cb2 = _dot_nt(cm, jnp.concatenate([bm, bm], axis=0))
            st = st_ref[g]
            y_off = _dot(cm, st.astype(BF16)) * ecum_x[:, gs]
            st_ref[g] = st * dec_x[:, gs] + _dot_tn(bm, xd_end[:, gs])
            pieces = []
            for p in range(SSD_GROUP_WIDTH // LANES):
                h0 = g * (SSD_HEADS // SSD_GROUPS) + 2 * p
                cs = slice(g * SSD_GROUP_WIDTH + p * LANES, g * SSD_GROUP_WIDTH + (p + 1) * LANES)
                col = cum_x[:, cs]
                row = jnp.where(left, cum2_t[h0:h0 + 1, :], cum2_t[h0 + 1:h0 + 2, :])
                decay = jnp.exp(jnp.where(tri2, col - row, -jnp.inf))
                m2 = (cb2 * decay).astype(BF16)
                xp = xdt_b[:, cs]
                zero = jnp.zeros_like(xp)
                x2 = jnp.concatenate([jnp.where(left, xp, zero), jnp.where(left, zero, xp)],
                                     axis=0)
                pieces.append(_dot(m2, x2))
            ys.append(jnp.concatenate(pieces, axis=1) + y_off)
        y = jnp.concatenate(ys, axis=1) + dx_ref[...] * xc
        y = y * _silu(z_ref[sl, :])
        nw = nw_ref[...]
        for g in range(SSD_GROUPS):
            gs = slice(g * SSD_GROUP_WIDTH, (g + 1) * SSD_GROUP_WIDTH)
            yg = y[:, gs]
            ms = jnp.mean(yg * yg, axis=-1, keepdims=True)
            o_ref[sl, gs] = yg * lax.rsqrt(ms + EPS) * nw[:, gs]
        return carry

    lax.fori_loop(0, ts // CHUNK, chunk, 0)


def _ssd(proj, dt_raw, conv_wx, conv_wb, conv_bx, conv_bb, dt_bias, a_log, d_x, norm_w,
         batch, seq, ts):
    n = proj.shape[0]
    nt = seq // ts
    width = SSD_GROUPS * SSD_GROUP_WIDTH
    bcw = 2 * SSD_GROUPS * SSD_STATE

    def full(shape):
        return pl.BlockSpec(shape, lambda b, t: (0, 0))

    return pl.pallas_call(
        functools.partial(_ssd_body, ts=ts),
        grid=(batch, nt),
        in_specs=[
            pl.BlockSpec((ts, width), lambda b, t: (b * nt + t, 4)),
            pl.BlockSpec((ts, width), lambda b, t: (b * nt + t, 5)),
            pl.BlockSpec((ts, bcw), lambda b, t: (b * nt + t, 12)),
            pl.BlockSpec((ts, LANES), lambda b, t: (b * nt + t, 0)),
            full((SSD_CONV, width)), full((SSD_CONV, bcw)),
            full((1, width)), full((1, bcw)),
            full((1, LANES)), full((1, LANES)),
            full((1, width)), full((1, width)),
        ],
        out_specs=pl.BlockSpec((ts, width), lambda b, t: (b * nt + t, 0)),
        out_shape=jax.ShapeDtypeStruct((n, width), F32),
        scratch_shapes=[
            pltpu.VMEM((ts + 8, width), F32),
            pltpu.VMEM((ts + 8, bcw), F32),
            pltpu.VMEM((ts, width), F32),
            pltpu.VMEM((ts, bcw), F32),
            pltpu.VMEM((SSD_GROUPS, SSD_STATE, SSD_GROUP_WIDTH), F32),
        ],
        compiler_params=pltpu.CompilerParams(
            dimension_semantics=("arbitrary", "arbitrary"),
            vmem_limit_bytes=VMEM_LIMIT_BYTES),
        name="ssd",
    )(proj, proj, proj, dt_raw, conv_wx, conv_wb, conv_bx, conv_bb, dt_bias, a_log, d_x, norm_w)


def _outproj_body(x_ref, hg_ref, ssd_ref, w1_ref, w2_ref, nw_ref, x2_ref, xn_ref):
    x2 = (x_ref[...] + _dot(hg_ref[...].astype(BF16), w1_ref[...])
          + _dot(ssd_ref[...].astype(BF16), w2_ref[...]))
    x2_ref[...] = x2
    ms = jnp.mean(x2 * x2, axis=-1, keepdims=True)
    xn_ref[...] = (x2 * lax.rsqrt(ms + EPS) * nw_ref[...]).astype(BF16)


def _outproj(x2d, o_hg, o_ssd, w1, w2, norm_w, tm):
    n = x2d.shape[0]
    row = pl.BlockSpec((tm, D_MODEL), lambda i: (i, 0))
    wfull = pl.BlockSpec((D_MODEL, D_MODEL), lambda i: (0, 0))
    return pl.pallas_call(
        _outproj_body,
        grid=(n // tm,),
        in_specs=[row, row, row, wfull, wfull, pl.BlockSpec((1, D_MODEL), lambda i: (0, 0))],
        out_specs=[row, row],
        out_shape=[jax.ShapeDtypeStruct((n, D_MODEL), F32),
                   jax.ShapeDtypeStruct((n, D_MODEL), BF16)],
        compiler_params=pltpu.CompilerParams(
            dimension_semantics=("arbitrary",), vmem_limit_bytes=VMEM_LIMIT_BYTES),
        name="outproj",
    )(x2d, o_hg, o_ssd, w1, w2, norm_w)


def _top_values(s, k):
    vals = []
    for _ in range(k):
        mx = jnp.max(s, axis=0, keepdims=True)
        vals.append(mx)
        s = jnp.where(s >= mx, -jnp.inf, s)
    return vals


def _peer_body(xn_ref, x2_ref, wq_ref, keys_ref, u_ref, vt_ref, fw_ref, o_ref,
               s2_ref, th_ref, e1_ref, e2_ref, acc_ref, ht_ref, at_ref, *, tt, eb, tl):
    e = pl.program_id(1)
    ib = eb // PEER_KEYS
    jh = PEER_KEYS // 2

    @pl.when(e == 0)
    def _():
        q = _dot(xn_ref[...], wq_ref[...]).astype(BF16)
        for h in range(PEER_HEADS):
            s = []
            for p in range(2):
                c0 = (2 * h + p) * PEER_KEYS
                s.append(_dot_nt(keys_ref[2 * h + p], q[:, c0:c0 + PEER_KEYS]))
            a = _top_values(s[0], PEER_TOPK)
            b = _top_values(s[1], PEER_TOPK)
            b16 = jnp.concatenate(b, axis=0)
            cand = jnp.concatenate([ak + b16 for ak in a], axis=0)
            tau = _top_values(cand, PEER_TOPK)[-1]
            m = a[0] + b[0]
            z = jnp.sum(jnp.where(cand >= tau, jnp.exp(cand - m), 0.0), axis=0, keepdims=True)
            s2_ref[h] = s[1]
            th_ref[h] = tau - s[0]
            e1_ref[h] = jnp.exp(s[0] - a[0])
            e2_ref[h] = jnp.exp(s[1] - b[0]) / z
        acc_ref[...] = jnp.zeros_like(acc_ref)

    ht_ref[...] = _dot_nt(u_ref[...], xn_ref[...])

    def rows(ii, carry):
        i = e * ib + ii
        for t0 in range(0, tt, tl):
            ts_ = slice(t0, t0 + tl)
            thr = [th_ref[h, pl.ds(i, 1), ts_] for h in range(PEER_HEADS)]
            c1 = [e1_ref[h, pl.ds(i, 1), ts_] for h in range(PEER_HEADS)]
            for j0 in range(0, PEER_KEYS, jh):
                js = slice(j0, j0 + jh)
                w = jnp.zeros((jh, tl), F32)
                for h in range(PEER_HEADS):
                    w = w + jnp.where(s2_ref[h, js, ts_] >= thr[h], e2_ref[h, js, ts_], 0.0) * c1[h]
                r = pl.ds(pl.multiple_of(ii * PEER_KEYS + j0, jh), jh)
                at_ref[r, ts_] = (_gelu(ht_ref[r, ts_]) * w).astype(BF16)
        return carry

    lax.fori_loop(0, ib, rows, 0)
    acc_ref[...] += _dot(vt_ref[...], at_ref[...])

    @pl.when(e == pl.num_programs(1) - 1)
    def _():
        y = x2_ref[...] + acc_ref[...].T
        ms = jnp.mean(y * y, axis=-1, keepdims=True)
        o_ref[...] = y * lax.rsqrt(ms + EPS) * fw_ref[...]


def _peer(xn, x2, wq, keys, u, vt, final_w, tt, eb, tl):
    n = xn.shape[0]
    n_exp = u.shape[0]
    tok = lambda i, e: (i, 0)
    const2 = lambda i, e: (0, 0)
    stat = pltpu.VMEM((PEER_HEADS, PEER_KEYS, tt), F32)
    return pl.pallas_call(
        functools.partial(_peer_body, tt=tt, eb=eb, tl=tl),
        grid=(n // tt, n_exp // eb),
        in_specs=[
            pl.BlockSpec((tt, D_MODEL), tok),
            pl.BlockSpec((tt, D_MODEL), tok),
            pl.BlockSpec(wq.shape, const2),
            pl.BlockSpec(keys.shape, lambda i, e: (0, 0, 0)),
            pl.BlockSpec((eb, D_MODEL), lambda i, e: (e, 0)),
            pl.BlockSpec((D_MODEL, eb), lambda i, e: (0, e)),
            pl.BlockSpec((1, D_MODEL), const2),
        ],
        out_specs=pl.BlockSpec((tt, D_MODEL), tok),
        out_shape=jax.ShapeDtypeStruct((n, D_MODEL), F32),
        scratch_shapes=[stat, stat, stat, stat,
                        pltpu.VMEM((D_MODEL, tt), F32),
                        pltpu.VMEM((eb, tt), F32),
                        pltpu.VMEM((eb, tt), BF16)],
        compiler_params=pltpu.CompilerParams(
            dimension_semantics=("arbitrary", "arbitrary"),
            vmem_limit_bytes=VMEM_LIMIT_BYTES),
        name="peer",
    )(xn, x2, wq, keys, u, vt, final_w)


def _pad_lanes(v):
    return jnp.pad(v, (0, LANES - v.shape[0])).reshape(1, LANES)


def _tile(n, want):
    return want if n % want == 0 else n


def kernel(x, norm_mix_w, w_in, hg_lb_logits, hg_norm_w, ssd_conv_w, ssd_conv_b,
           ssd_dt_bias, ssd_A_log, ssd_D, ssd_norm_w, w_out, norm_ffn_w,
           peer_w_q, peer_keys, peer_u, peer_v, final_norm_w):
    batch, seq, d = x.shape
    n = batch * seq
    x2d = x.reshape(n, d)
    width = SSD_GROUPS * SSD_GROUP_WIDTH

    w_main = w_in[0][:, :PROJ_MAIN].astype(BF16)
    w_dt = jnp.pad(w_in[0][:, PROJ_MAIN:], ((0, 0), (0, LANES - SSD_HEADS))).astype(BF16)
    conv_w = ssd_conv_w[0]
    conv_b = ssd_conv_b[0].reshape(1, -1)
    w_out1 = w_out[0][:HG_HEADS * HG_HEAD_DIM].astype(BF16)
    w_out2 = w_out[0][HG_HEADS * HG_HEAD_DIM:].astype(BF16)
    keys = peer_keys[0].reshape(PEER_HEADS * 2, PEER_KEYS, -1).astype(BF16)
    u = peer_u[0].astype(BF16)
    vt = peer_v[0].astype(BF16).T

    proj, dt_raw = _inproj(x2d, norm_mix_w[0].reshape(1, d), w_main, w_dt,
                           _tile(n, 1024), PROJ_MAIN // 4)
    o_hg = _hgrn2(proj, hg_lb_logits, hg_norm_w[0].reshape(1, -1), batch, seq, _tile(seq, 512))
    o_ssd = _ssd(proj, dt_raw, conv_w[:, :width], conv_w[:, width:], conv_b[:, :width],
                 conv_b[:, width:], _pad_lanes(ssd_dt_bias[0]), _pad_lanes(ssd_A_log[0]),
                 jnp.repeat(ssd_D[0], SSD_HEAD_DIM).reshape(1, width),
                 ssd_norm_w[0].reshape(1, width), batch, seq, _tile(seq, 256))
    x2, xn = _outproj(x2d, o_hg, o_ssd, w_out1, w_out2, norm_ffn_w[0].reshape(1, d),
                      _tile(n, 512))
    out = _peer(xn, x2, peer_w_q[0].astype(BF16), keys, u, vt, final_norm_w.reshape(1, d),
                _tile(n, 512), 1024, 256)
    return out.reshape(batch, seq, d)
```

```python
import functools
import math

import jax
import jax.numpy as jnp
from jax import lax
from jax.experimental import pallas as pl
from jax.experimental.pallas import tpu as pltpu

F32 = jnp.float32
BF16 = jnp.bfloat16
EPS = 1e-6
HIGHEST = lax.Precision.HIGHEST

D_MODEL = 1024
HG_HEADS = 8
HG_HEAD_DIM = 128
CHUNK = 64
SSD_HEADS = 16
SSD_HEAD_DIM = 64
SSD_GROUPS = 2
SSD_STATE = 128
SSD_CONV = 4
SSD_GROUP_WIDTH = 512
PEER_HEADS = 8
PEER_KEYS = 128
PEER_TOPK = 16
PROJ_MAIN = 6656
LANES = 128

VMEM_LIMIT_BYTES = 56 * 1024 * 1024


def _dot(a, b):
    return jnp.dot(a, b, preferred_element_type=F32)


def _dot_nt(a, b):
    return lax.dot_general(a, b, (((1,), (1,)), ((), ())), preferred_element_type=F32)


def _dot_tn(a, b):
    return lax.dot_general(a, b, (((0,), (0,)), ((), ())), preferred_element_type=F32)


def _dot_exact(a, b):
    return jnp.dot(a, b, preferred_element_type=F32, precision=HIGHEST)


def _sigmoid(x):
    return 1.0 / (1.0 + jnp.exp(-x))


def _silu(x):
    return x * _sigmoid(x)


def _softplus(x):
    return jnp.maximum(x, 0.0) + jnp.log(1.0 + jnp.exp(-jnp.abs(x)))


def _gelu(x):
    return 0.5 * x * (1.0 + lax.erf(x * (1.0 / math.sqrt(2.0))))


def _tri(n, m=None):
    m = n if m is None else m
    r = lax.broadcasted_iota(jnp.int32, (n, m), 0)
    c = lax.broadcasted_iota(jnp.int32, (n, m), 1)
    return r >= (c % n)


def _inproj_body(x_ref, nw_ref, w_ref, wdt_ref, o_ref, dt_ref, hn_ref):
    @pl.when(pl.program_id(1) == 0)
    def _():
        x = x_ref[...]
        ms = jnp.mean(x * x, axis=-1, keepdims=True)
        hn = (x * lax.rsqrt(ms + EPS) * nw_ref[...]).astype(BF16)
        hn_ref[...] = hn
        dt_ref[...] = _dot(hn, wdt_ref[...])

    o_ref[...] = _dot(hn_ref[...], w_ref[...]).astype(o_ref.dtype)


def _inproj(x2d, norm_w, w_main, w_dt, tm, tn):
    n = x2d.shape[0]
    return pl.pallas_call(
        _inproj_body,
        grid=(n // tm, PROJ_MAIN // tn),
        in_specs=[
            pl.BlockSpec((tm, D_MODEL), lambda i, j: (i, 0)),
            pl.BlockSpec((1, D_MODEL), lambda i, j: (0, 0)),
            pl.BlockSpec((D_MODEL, tn), lambda i, j: (0, j)),
            pl.BlockSpec((D_MODEL, LANES), lambda i, j: (0, 0)),
        ],
        out_specs=[
            pl.BlockSpec((tm, tn), lambda i, j: (i, j)),
            pl.BlockSpec((tm, LANES), lambda i, j: (i, 0)),
        ],
        out_shape=[
            jax.ShapeDtypeStruct((n, PROJ_MAIN), F32),
            jax.ShapeDtypeStruct((n, LANES), F32),
        ],
        scratch_shapes=[pltpu.VMEM((tm, D_MODEL), BF16)],
        compiler_params=pltpu.CompilerParams(
            dimension_semantics=("arbitrary", "arbitrary"),
            vmem_limit_bytes=VMEM_LIMIT_BYTES),
        name="inproj",
    )(x2d, norm_w, w_main, w_dt)


def _hgrn2_body(q_ref, f_ref, i_ref, g_ref, lbl_ref, nw_ref, o_ref, st_ref, *, n_chunks):
    @pl.when(pl.program_id(2) == 0)
    def _():
        st_ref[...] = jnp.zeros_like(st_ref)

    lbl = lbl_ref[...]
    e = jnp.exp(lbl - jnp.max(lbl, axis=0, keepdims=True))
    lb = e[0:1, :] / jnp.sum(e, axis=0, keepdims=True)
    tri = _tri(CHUNK)
    tri_f = tri.astype(F32)
    nw = nw_ref[...]

    def chunk(c, carry):
        sl = pl.ds(pl.multiple_of(c * CHUNK, CHUNK), CHUNK)
        f = lb + (1.0 - lb) * _sigmoid(f_ref[sl, :])
        k = 1.0 - f
        g_cum = _dot_exact(tri_f, jnp.log(f))
        g_last = g_cum[CHUNK - 1:CHUNK, :]
        q_dec = (q_ref[sl, :] * jnp.exp(g_cum)).astype(BF16)
        k_dec = (k * jnp.exp(-g_cum)).astype(BF16)
        v = i_ref[sl, :].astype(BF16)
        scores = jnp.where(tri, _dot_nt(q_dec, k_dec), 0.0)
        st = st_ref[...]
        o = _dot(scores.astype(BF16), v) + _dot_nt(q_dec, st.astype(BF16))
        k_end = (k * jnp.exp(g_last - g_cum)).astype(BF16)
        st_ref[...] = st * jnp.exp(g_last) + _dot_tn(v, k_end)
        ms = jnp.mean(o * o, axis=-1, keepdims=True)
        o_ref[sl, :] = o * lax.rsqrt(ms + EPS) * nw * _silu(g_ref[sl, :])
        return carry

    lax.fori_loop(0, n_chunks, chunk, 0)


def _hgrn2(proj, lb_logits, norm_w, batch, seq, ts):
    n = proj.shape[0]
    nt = seq // ts
    hd = HG_HEAD_DIM

    def col(k):
        return pl.BlockSpec((ts, hd), lambda b, h, t: (b * nt + t, k * HG_HEADS + h))

    return pl.pallas_call(
        functools.partial(_hgrn2_body, n_chunks=ts // CHUNK),
        grid=(batch, HG_HEADS, nt),
        in_specs=[
            col(0), col(1), col(2), col(3),
            pl.BlockSpec((2, hd), lambda b, h, t: (0, h)),
            pl.BlockSpec((1, hd), lambda b, h, t: (0, 0)),
        ],
        out_specs=pl.BlockSpec((ts, hd), lambda b, h, t: (b * nt + t, h)),
        out_shape=jax.ShapeDtypeStruct((n, HG_HEADS * hd), F32),
        scratch_shapes=[pltpu.VMEM((hd, hd), F32)],
        compiler_params=pltpu.CompilerParams(
            dimension_semantics=("arbitrary", "arbitrary", "arbitrary"),
            vmem_limit_bytes=VMEM_LIMIT_BYTES),
        name="hgrn2",
    )(proj, proj, proj, proj, lb_logits, norm_w)


def _ssd_body(z_ref, xs_ref, bc_ref, dt_ref, cwx_ref, cwb_ref, cbx_ref, cbb_ref,
              dtb_ref, alog_ref, dx_ref, nw_ref, o_ref,
              xpad_ref, bpad_ref, xc_ref, bcc_ref, st_ref, *, ts):
    width = SSD_GROUPS * SSD_GROUP_WIDTH
    bcw = 2 * SSD_GROUPS * SSD_STATE

    @pl.when(pl.program_id(1) == 0)
    def _():
        st_ref[...] = jnp.zeros_like(st_ref)
        xpad_ref[0:8, :] = jnp.zeros((8, width), F32)
        bpad_ref[0:8, :] = jnp.zeros((8, bcw), F32)

    xpad_ref[8:, :] = xs_ref[...]
    bpad_ref[8:, :] = bc_ref[...]
    accx = cbx_ref[...]
    accb = cbb_ref[...]
    for j in range(SSD_CONV):
        lo = 8 - (SSD_CONV - 1) + j
        accx = accx + xpad_ref[lo:lo + ts, :] * cwx_ref[j:j + 1, :]
        accb = accb + bpad_ref[lo:lo + ts, :] * cwb_ref[j:j + 1, :]
    xc_ref[...] = _silu(accx)
    bcc_ref[...] = _silu(accb)
    xpad_ref[0:8, :] = xs_ref[ts - 8:ts, :]
    bpad_ref[0:8, :] = bc_ref[ts - 8:ts, :]

    a_row = -jnp.exp(alog_ref[...])
    dtb = dtb_ref[...]
    tri_f = _tri(CHUNK).astype(F32)
    tri2 = _tri(CHUNK, 2 * CHUNK)
    lane = lax.broadcasted_iota(jnp.int32, (1, LANES), 1)
    left = lane < SSD_HEAD_DIM
    hrow = lax.broadcasted_iota(jnp.int32, (LANES, width), 0)
    hcol = lax.broadcasted_iota(jnp.int32, (LANES, width), 1) // SSD_HEAD_DIM
    expand = (hrow == hcol).astype(F32)

    def chunk(c, carry):
        sl = pl.ds(pl.multiple_of(c * CHUNK, CHUNK), CHUNK)
        dt = _softplus(dt_ref[sl, :] + dtb)
        cum = _dot_exact(tri_f, dt * a_row)
        cum_x = _dot_exact(cum, expand)
        dt_x = _dot_exact(dt, expand)
        cum2 = jnp.concatenate([cum, cum], axis=0)
        cum2_t = cum2.T
        xc = xc_ref[sl, :]
        xdt = xc * dt_x
        last_x = cum_x[CHUNK - 1:CHUNK, :]
        xd_end = (xdt * jnp.exp(last_x - cum_x)).astype(BF16)
        ecum_x = jnp.exp(cum_x)
        dec_x = jnp.exp(last_x)
        xdt_b = xdt.astype(BF16)
        ys = []
        for g in range(SSD_GROUPS):
            gs = slice(g * SSD_GROUP_WIDTH, (g + 1) * SSD_GROUP_WIDTH)
            bm = bcc_ref[sl, g * SSD_STATE:(g + 1) * SSD_STATE].astype(BF16)
            cm = bcc_ref[sl, (SSD_GROUPS + g) * SSD_STATE:
                         (SSD_GROUPS + g + 1) * SSD_STATE].astype(BF16)
            cb2 = _dot_nt(cm, jnp.concatenate([bm, bm], axis=0))
            st = st_ref[g]
            y_off = _dot(cm, st.astype(BF16)) * ecum_x[:, gs]
            st_ref[g] = st * dec_x[:, gs] + _dot_tn(bm, xd_end[:, gs])
            pieces = []
            for p in range(SSD_GROUP_WIDTH // LANES):
                h0 = g * (SSD_HEADS // SSD_GROUPS) + 2 * p
                cs = slice(g * SSD_GROUP_WIDTH + p * LANES, g * SSD_GROUP_WIDTH + (p + 1) * LANES)
                col = cum_x[:, cs]
                row = jnp.where(left, cum2_t[h0:h0 + 1, :], cum2_t[h0 + 1:h0 + 2, :])
                decay = jnp.exp(jnp.where(tri2, col - row, -jnp.inf))
                m2 = (cb2 * decay).astype(BF16)
                xp = xdt_b[:, cs]
                zero = jnp.zeros_like(xp)
                x2 = jnp.concatenate([jnp.where(left, xp, zero), jnp.where(left, zero, xp)],
                                     axis=0)
                pieces.append(_dot(m2, x2))
            ys.append(jnp.concatenate(pieces, axis=1) + y_off)
        y = jnp.concatenate(ys, axis=1) + dx_ref[...] * xc
        y = y * _silu(z_ref[sl, :])
        nw = nw_ref[...]
        for g in range(SSD_GROUPS):
            gs = slice(g * SSD_GROUP_WIDTH, (g + 1) * SSD_GROUP_WIDTH)
            yg = y[:, gs]
            ms = jnp.mean(yg * yg, axis=-1, keepdims=True)
            o_ref[sl, gs] = yg * lax.rsqrt(ms + EPS) * nw[:, gs]
        return carry

    lax.fori_loop(0, ts // CHUNK, chunk, 0)


def _ssd(proj, dt_raw, conv_wx, conv_wb, conv_bx, conv_bb, dt_bias, a_log, d_x, norm_w,
         batch, seq, ts):
    n = proj.shape[0]
    nt = seq // ts
    width = SSD_GROUPS * SSD_GROUP_WIDTH
    bcw = 2 * SSD_GROUPS * SSD_STATE

    def full(shape):
        return pl.BlockSpec(shape, lambda b, t: (0, 0))

    return pl.pallas_call(
        functools.partial(_ssd_body, ts=ts),
        grid=(batch, nt),
        in_specs=[
            pl.BlockSpec((ts, width), lambda b, t: (b * nt + t, 4)),
            pl.BlockSpec((ts, width), lambda b, t: (b * nt + t, 5)),
            pl.BlockSpec((ts, bcw), lambda b, t: (b * nt + t, 12)),
            pl.BlockSpec((ts, LANES), lambda b, t: (b * nt + t, 0)),
            full((SSD_CONV, width)), full((SSD_CONV, bcw)),
            full((1, width)), full((1, bcw)),
            full((1, LANES)), full((1, LANES)),
            full((1, width)), full((1, width)),
        ],
        out_specs=pl.BlockSpec((ts, width), lambda b, t: (b * nt + t, 0)),
        out_shape=jax.ShapeDtypeStruct((n, width), F32),
        scratch_shapes=[
            pltpu.VMEM((ts + 8, width), F32),
            pltpu.VMEM((ts + 8, bcw), F32),
            pltpu.VMEM((ts, width), F32),
            pltpu.VMEM((ts, bcw), F32),
            pltpu.VMEM((SSD_GROUPS, SSD_STATE, SSD_GROUP_WIDTH), F32),
        ],
        compiler_params=pltpu.CompilerParams(
            dimension_semantics=("arbitrary", "arbitrary"),
            vmem_limit_bytes=VMEM_LIMIT_BYTES),
        name="ssd",
    )(proj, proj, proj, dt_raw, conv_wx, conv_wb, conv_bx, conv_bb, dt_bias, a_log, d_x, norm_w)


def _outproj_body(x_ref, hg_ref, ssd_ref, w1_ref, w2_ref, nw_ref, x2_ref, xn_ref):
    x2 = (x_ref[...] + _dot(hg_ref[...].astype(BF16), w1_ref[...])
          + _dot(ssd_ref[...].astype(BF16), w2_ref[...]))
    x2_ref[...] = x2
    ms = jnp.mean(x2 * x2, axis=-1, keepdims=True)
    xn_ref[...] = (x2 * lax.rsqrt(ms + EPS) * nw_ref[...]).astype(BF16)


def _outproj(x2d, o_hg, o_ssd, w1, w2, norm_w, tm):
    n = x2d.shape[0]
    row = pl.BlockSpec((tm, D_MODEL), lambda i: (i, 0))
    wfull = pl.BlockSpec((D_MODEL, D_MODEL), lambda i: (0, 0))
    return pl.pallas_call(
        _outproj_body,
        grid=(n // tm,),
        in_specs=[row, row, row, wfull, wfull, pl.BlockSpec((1, D_MODEL), lambda i: (0, 0))],
        out_specs=[row, row],
        out_shape=[jax.ShapeDtypeStruct((n, D_MODEL), F32),
                   jax.ShapeDtypeStruct((n, D_MODEL), BF16)],
        compiler_params=pltpu.CompilerParams(
            dimension_semantics=("arbitrary",), vmem_limit_bytes=VMEM_LIMIT_BYTES),
        name="outproj",
    )(x2d, o_hg, o_ssd, w1, w2, norm_w)


def _top_values(s, k):
    vals = []
    for _ in range(k):
        mx = jnp.max(s, axis=0, keepdims=True)
        vals.append(mx)
        s = jnp.where(s >= mx, -jnp.inf, s)
    return vals


def _candidate_sums(a, b):
    a16 = jnp.concatenate(a, axis=0)
    b16 = jnp.concatenate(b, axis=0)
    b8 = b16[0:8]
    rank = lax.broadcasted_iota(jnp.int32, b8.shape, 0) + 1
    rows = [a[0] + b16, a[1] + b8]
    for k in range(3, 9):
        rows.append(jnp.where(rank * k <= PEER_TOPK, a[k - 1] + b8, -jnp.inf))
    rows.append(a16[8:16] + b[0])
    return jnp.concatenate(rows, axis=0)


def _top_values_ranked(s, k):
    vals = []
    rank = jnp.full(s.shape, 2.0 * k, F32)
    for r in range(k):
        mx = jnp.max(s, axis=0, keepdims=True)
        vals.append(mx)
        hit = s >= mx
        rank = jnp.where(hit, r + 1.0, rank)
        s = jnp.where(hit, -jnp.inf, s)
    return vals, rank


def _pair_of_bf16(x):
    bits = lax.bitcast_convert_type(x.astype(BF16).astype(F32), jnp.uint32) >> 16
    return bits | (bits << 16)


def _packed_row(ref, k, i):
    row = jnp.broadcast_to(ref[k, pl.ds(i, 1), :], (8, LANES))
    return pltpu.bitcast(row, BF16)


def _peer_body(xn_ref, x2_ref, wq_ref, keys_ref, u_ref, vt_ref, fw_ref, o_ref,
               s1_ref, s2_ref, n_ref, c1_ref, r2_ref, e2_ref, acc_ref, *, tt, eb, eq):
    e = pl.program_id(1)
    ib = eb // PEER_KEYS
    n_tc = tt // LANES
    pk = 16

    @pl.when(e == 0)
    def _():
        q = _dot(xn_ref[...], wq_ref[...]).astype(BF16)
        for h in range(PEER_HEADS):
            for p, ref in ((0, s1_ref), (1, s2_ref)):
                c0 = (2 * h + p) * PEER_KEYS
                s = _dot_nt(keys_ref[2 * h + p], q[:, c0:c0 + PEER_KEYS])
                for tc in range(n_tc):
                    ref[h * n_tc + tc] = s[:, tc * LANES:(tc + 1) * LANES]

        def stats(k, carry):
            s1 = s1_ref[k]
            s2 = s2_ref[k]
            a = _top_values(s1, PEER_TOPK)
            b, rank2 = _top_values_ranked(s2, PEER_TOPK)
            cand = _candidate_sums(a, b)
            tau = _top_values(cand, PEER_TOPK)[-1]
            m = a[0] + b[0]
            z = jnp.sum(jnp.where(cand >= tau, jnp.exp(cand - m), 0.0), axis=0, keepdims=True)
            count = jnp.zeros_like(s1)
            for l in range(PEER_TOPK):
                count = jnp.where(s1 + b[l] >= tau, l + 1.0, count)
            n_ref[k] = _pair_of_bf16(count)
            c1_ref[k] = _pair_of_bf16(jnp.exp(s1 - a[0]))
            r2_ref[k] = rank2.astype(BF16)
            e2_ref[k] = (jnp.exp(s2 - b[0]) / z).astype(BF16)
            return carry

        lax.fori_loop(0, PEER_HEADS * n_tc, stats, 0)
        acc_ref[...] = jnp.zeros_like(acc_ref)

    xn = xn_ref[...]

    def pre_activations(r0):
        return _dot_nt(u_ref[r0:r0 + eq, :], xn)

    hq_next = pre_activations(0)
    at_prev = None
    for r0 in range(0, eb, eq):
        hq = hq_next
        if r0 + eq < eb:
            hq_next = pre_activations(r0 + eq)
        if at_prev is not None:
            acc_ref[...] += _dot(vt_ref[:, r0 - eq:r0], at_prev)
        n_ii = eq // PEER_KEYS
        n_jv = PEER_KEYS // pk
        zero = jnp.zeros((pk, LANES), BF16)
        blocks = [[None] * n_tc for _ in range(n_ii)]
        for tc in range(n_tc):
            w = [[None] * n_jv for _ in range(n_ii)]
            for h in range(PEER_HEADS):
                k = h * n_tc + tc
                rank2 = [r2_ref[k, jv * pk:(jv + 1) * pk, :] for jv in range(n_jv)]
                e2 = [e2_ref[k, jv * pk:(jv + 1) * pk, :] for jv in range(n_jv)]
                for ii in range(n_ii):
                    i = e * ib + r0 // PEER_KEYS + ii
                    count = _packed_row(n_ref, k, i)
                    c1 = _packed_row(c1_ref, k, i)
                    for jv in range(n_jv):
                        g = jnp.where(rank2[jv] <= count, e2[jv], zero) * c1
                        w[ii][jv] = g if h == 0 else w[ii][jv] + g
            for ii in range(n_ii):
                hh = hq[ii * PEER_KEYS:(ii + 1) * PEER_KEYS, tc * LANES:(tc + 1) * LANES]
                blocks[ii][tc] = _gelu(hh).astype(BF16) * jnp.concatenate(w[ii], axis=0)
        at_prev = jnp.concatenate([jnp.concatenate(row, axis=1) for row in blocks],
                                  axis=0)
    acc_ref[...] += _dot(vt_ref[:, eb - eq:eb], at_prev)

    @pl.when(e == pl.num_programs(1) - 1)
    def _():
        y = x2_ref[...] + acc_ref[...].T
        ms = jnp.mean(y * y, axis=-1, keepdims=True)
        o_ref[...] = y * lax.rsqrt(ms + EPS) * fw_ref[...]


def _peer(xn, x2, wq, keys, u, vt, final_w, tt, eb, eq):
    n = xn.shape[0]
    n_exp = u.shape[0]
    tok = lambda i, e: (i, 0)
    const2 = lambda i, e: (0, 0)
    def stat(dtype):
        return pltpu.VMEM((PEER_HEADS * (tt // LANES), PEER_KEYS, LANES), dtype)

    return pl.pallas_call(
        functools.partial(_peer_body, tt=tt, eb=eb, eq=eq),
        grid=(n // tt, n_exp // eb),
        in_specs=[
            pl.BlockSpec((tt, D_MODEL), tok),
            pl.BlockSpec((tt, D_MODEL), tok),
            pl.BlockSpec(wq.shape, const2),
            pl.BlockSpec(keys.shape, lambda i, e: (0, 0, 0)),
            pl.BlockSpec((eb, D_MODEL), lambda i, e: (e, 0)),
            pl.BlockSpec((D_MODEL, eb), lambda i, e: (0, e)),
            pl.BlockSpec((1, D_MODEL), const2),
        ],
        out_specs=pl.BlockSpec((tt, D_MODEL), tok),
        out_shape=jax.ShapeDtypeStruct((n, D_MODEL), F32),
        scratch_shapes=[stat(F32), stat(F32), stat(jnp.uint32), stat(jnp.uint32),
                        stat(BF16), stat(BF16), pltpu.VMEM((D_MODEL, tt), F32)],
        compiler_params=pltpu.CompilerParams(
            dimension_semantics=("arbitrary", "arbitrary"),
            vmem_limit_bytes=VMEM_LIMIT_BYTES),
        name="peer",
    )(xn, x2, wq, keys, u, vt, final_w)


def _pad_lanes(v):
    return jnp.pad(v, (0, LANES - v.shape[0])).reshape(1, LANES)


def _tile(n, want):
    return want if n % want == 0 else n


def kernel(x, norm_mix_w, w_in, hg_lb_logits, hg_norm_w, ssd_conv_w, ssd_conv_b,
           ssd_dt_bias, ssd_A_log, ssd_D, ssd_norm_w, w_out, norm_ffn_w,
           peer_w_q, peer_keys, peer_u, peer_v, final_norm_w):
    batch, seq, d = x.shape
    n = batch * seq
    x2d = x.reshape(n, d)
    width = SSD_GROUPS * SSD_GROUP_WIDTH

    w_main = w_in[0][:, :PROJ_MAIN].astype(BF16)
    w_dt = jnp.pad(w_in[0][:, PROJ_MAIN:], ((0, 0), (0, LANES - SSD_HEADS))).astype(BF16)
    conv_w = ssd_conv_w[0]
    conv_b = ssd_conv_b[0].reshape(1, -1)
    w_out1 = w_out[0][:HG_HEADS * HG_HEAD_DIM].astype(BF16)
    w_out2 = w_out[0][HG_HEADS * HG_HEAD_DIM:].astype(BF16)
    keys = peer_keys[0].reshape(PEER_HEADS * 2, PEER_KEYS, -1).astype(BF16)
    u = peer_u[0].astype(BF16)
    vt = peer_v[0].astype(BF16).T

    proj, dt_raw = _inproj(x2d, norm_mix_w[0].reshape(1, d), w_main, w_dt,
                           _tile(n, 1024), PROJ_MAIN // 4)
    o_hg = _hgrn2(proj, hg_lb_logits, hg_norm_w[0].reshape(1, -1), batch, seq, _tile(seq, 512))
    o_ssd = _ssd(proj, dt_raw, conv_w[:, :width], conv_w[:, width:], conv_b[:, :width],
                 conv_b[:, width:], _pad_lanes(ssd_dt_bias[0]), _pad_lanes(ssd_A_log[0]),
                 jnp.repeat(ssd_D[0], SSD_HEAD_DIM).reshape(1, width),
                 ssd_norm_w[0].reshape(1, width), batch, seq, _tile(seq, 256))
    x2, xn = _outproj(x2d, o_hg, o_ssd, w_out1, w_out2, norm_ffn_w[0].reshape(1, d),
                      _tile(n, 512))
    out = _peer(xn, x2, peer_w_q[0].astype(BF16), keys, u, vt, final_norm_w.reshape(1, d),
                _tile(n, 512), 1024, 256)
    return out.reshape(batch, seq, d)
```

```python
import functools
import math

import jax
import jax.numpy as jnp
from jax import lax
from jax.experimental import pallas as pl
from jax.experimental.pallas import tpu as pltpu

F32 = jnp.float32
BF16 = jnp.bfloat16
EPS = 1e-6
HIGHEST = lax.Precision.HIGHEST

D_MODEL = 1024
HG_HEADS = 8
HG_HEAD_DIM = 128
CHUNK = 64
SSD_HEADS = 16
SSD_HEAD_DIM = 64
SSD_GROUPS = 2
SSD_STATE = 128
SSD_CONV = 4
SSD_GROUP_WIDTH = 512
PEER_HEADS = 8
PEER_KEYS = 128
PEER_TOPK = 16
PROJ_MAIN = 6656
LANES = 128

VMEM_LIMIT_BYTES = 56 * 1024 * 1024


def _dot(a, b):
    return jnp.dot(a, b, preferred_element_type=F32)


def _dot_nt(a, b):
    return lax.dot_general(a, b, (((1,), (1,)), ((), ())), preferred_element_type=F32)


def _dot_tn(a, b):
    return lax.dot_general(a, b, (((0,), (0,)), ((), ())), preferred_element_type=F32)


def _dot_exact(a, b):
    return jnp.dot(a, b, preferred_element_type=F32, precision=HIGHEST)


def _sigmoid(x):
    return 1.0 / (1.0 + jnp.exp(-x))


def _silu(x):
    return x * _sigmoid(x)


def _softplus(x):
    return jnp.maximum(x, 0.0) + jnp.log(1.0 + jnp.exp(-jnp.abs(x)))


def _gelu(x):
    return 0.5 * x * (1.0 + lax.erf(x * (1.0 / math.sqrt(2.0))))


def _tri(n, m=None):
    m = n if m is None else m
    r = lax.broadcasted_iota(jnp.int32, (n, m), 0)
    c = lax.broadcasted_iota(jnp.int32, (n, m), 1)
    return r >= (c % n)


def _inproj_body(x_ref, nw_ref, w_ref, wdt_ref, o_ref, dt_ref, hn_ref):
    @pl.when(pl.program_id(1) == 0)
    def _():
        x = x_ref[...]
        ms = jnp.mean(x * x, axis=-1, keepdims=True)
        hn = (x * lax.rsqrt(ms + EPS) * nw_ref[...]).astype(BF16)
        hn_ref[...] = hn
        dt_ref[...] = _dot(hn, wdt_ref[...])

    o_ref[...] = _dot(hn_ref[...], w_ref[...]).astype(o_ref.dtype)


def _inproj(x2d, norm_w, w_main, w_dt, tm, tn):
    n = x2d.shape[0]
    return pl.pallas_call(
        _inproj_body,
        grid=(n // tm, PROJ_MAIN // tn),
        in_specs=[
            pl.BlockSpec((tm, D_MODEL), lambda i, j: (i, 0)),
            pl.BlockSpec((1, D_MODEL), lambda i, j: (0, 0)),
            pl.BlockSpec((D_MODEL, tn), lambda i, j: (0, j)),
            pl.BlockSpec((D_MODEL, LANES), lambda i, j: (0, 0)),
        ],
        out_specs=[
            pl.BlockSpec((tm, tn), lambda i, j: (i, j)),
            pl.BlockSpec((tm, LANES), lambda i, j: (i, 0)),
        ],
        out_shape=[
            jax.ShapeDtypeStruct((n, PROJ_MAIN), BF16),
            jax.ShapeDtypeStruct((n, LANES), F32),
        ],
        scratch_shapes=[pltpu.VMEM((tm, D_MODEL), BF16)],
        compiler_params=pltpu.CompilerParams(
            dimension_semantics=("arbitrary", "arbitrary"),
            vmem_limit_bytes=VMEM_LIMIT_BYTES),
        name="inproj",
    )(x2d, norm_w, w_main, w_dt)


def _hgrn2_body(q_ref, f_ref, i_ref, g_ref, lbl_ref, nw_ref, o_ref, st_ref, *, n_chunks):
    @pl.when(pl.program_id(1) == 0)
    def _():
        st_ref[...] = jnp.zeros_like(st_ref)

    lbl = lbl_ref[...]
    e = jnp.exp(lbl - jnp.max(lbl, axis=0, keepdims=True))
    lb = e[0:1, :] / jnp.sum(e, axis=0, keepdims=True)
    tri = _tri(CHUNK)
    tri_f = tri.astype(F32)
    nw = nw_ref[...]
    hd = HG_HEAD_DIM

    def chunk(c, carry):
        sl = pl.ds(pl.multiple_of(c * CHUNK, CHUNK), CHUNK)
        f = lb + (1.0 - lb) * _sigmoid(f_ref[sl, :].astype(F32))
        k = 1.0 - f
        g_cum = _dot_exact(tri_f, jnp.log(f))
        g_last = g_cum[CHUNK - 1:CHUNK, :]
        q_dec = (q_ref[sl, :].astype(F32) * jnp.exp(g_cum)).astype(BF16)
        k_dec = (k * jnp.exp(-g_cum)).astype(BF16)
        k_end = (k * jnp.exp(g_last - g_cum)).astype(BF16)
        decay = jnp.exp(g_last)
        v = i_ref[sl, :]
        gate = _silu(g_ref[sl, :].astype(F32))
        for h in range(HG_HEADS):
            hs = slice(h * hd, (h + 1) * hd)
            scores = jnp.where(tri, _dot_nt(q_dec[:, hs], k_dec[:, hs]), 0.0)
            st = st_ref[h]
            o = _dot(scores.astype(BF16), v[:, hs]) + _dot_nt(q_dec[:, hs], st.astype(BF16))
            st_ref[h] = st * decay[:, hs] + _dot_tn(v[:, hs], k_end[:, hs])
            ms = jnp.mean(o * o, axis=-1, keepdims=True)
            o_ref[sl, hs] = (o * lax.rsqrt(ms + EPS) * nw * gate[:, hs]).astype(o_ref.dtype)
        return carry

    lax.fori_loop(0, n_chunks, chunk, 0)


def _hgrn2(proj, lb_logits, norm_w, batch, seq, ts):
    n = proj.shape[0]
    nt = seq // ts
    hd = HG_HEAD_DIM
    width = HG_HEADS * hd

    def col(k):
        return pl.BlockSpec((ts, width), lambda b, t: (b * nt + t, k))

    return pl.pallas_call(
        functools.partial(_hgrn2_body, n_chunks=ts // CHUNK),
        grid=(batch, nt),
        in_specs=[
            col(0), col(1), col(2), col(3),
            pl.BlockSpec((2, width), lambda b, t: (0, 0)),
            pl.BlockSpec((1, hd), lambda b, t: (0, 0)),
        ],
        out_specs=pl.BlockSpec((ts, width), lambda b, t: (b * nt + t, 0)),
        out_shape=jax.ShapeDtypeStruct((n, width), BF16),
        scratch_shapes=[pltpu.VMEM((HG_HEADS, hd, hd), F32)],
        compiler_params=pltpu.CompilerParams(
            dimension_semantics=("arbitrary", "arbitrary"),
            vmem_limit_bytes=VMEM_LIMIT_BYTES),
        name="hgrn2",
    )(proj, proj, proj, proj, lb_logits, norm_w)


def _ssd_body(z_ref, xs_ref, bc_ref, dt_ref, cwx_ref, cwb_ref, cbx_ref, cbb_ref,
              dtb_ref, alog_ref, dx_ref, nw_ref, o_ref,
              xpad_ref, bpad_ref, xc_ref, bcc_ref, st_ref, *, ts):
    width = SSD_GROUPS * SSD_GROUP_WIDTH
    bcw = 2 * SSD_GROUPS * SSD_STATE

    @pl.when(pl.program_id(1) == 0)
    def _():
        st_ref[...] = jnp.zeros_like(st_ref)
        xpad_ref[0:8, :] = jnp.zeros((8, width), F32)
        bpad_ref[0:8, :] = jnp.zeros((8, bcw), F32)

    xpad_ref[8:, :] = xs_ref[...].astype(F32)
    bpad_ref[8:, :] = bc_ref[...].astype(F32)
    accx = cbx_ref[...]
    accb = cbb_ref[...]
    for j in range(SSD_CONV):
        lo = 8 - (SSD_CONV - 1) + j
        accx = accx + xpad_ref[lo:lo + ts, :] * cwx_ref[j:j + 1, :]
        accb = accb + bpad_ref[lo:lo + ts, :] * cwb_ref[j:j + 1, :]
    xc_ref[...] = _silu(accx)
    bcc_ref[...] = _silu(accb)
    xpad_ref[0:8, :] = xpad_ref[ts:ts + 8, :]
    bpad_ref[0:8, :] = bpad_ref[ts:ts + 8, :]

    a_row = -jnp.exp(alog_ref[...])
    dtb = dtb_ref[...]
    tri_f = _tri(CHUNK).astype(F32)
    tri2 = _tri(CHUNK, 2 * CHUNK)
    lane = lax.broadcasted_iota(jnp.int32, (1, LANES), 1)
    left = lane < SSD_HEAD_DIM
    hrow = lax.broadcasted_iota(jnp.int32, (LANES, width), 0)
    hcol = lax.broadcasted_iota(jnp.int32, (LANES, width), 1) // SSD_HEAD_DIM
    expand = (hrow == hcol).astype(F32)

    def chunk(c, carry):
        sl = pl.ds(pl.multiple_of(c * CHUNK, CHUNK), CHUNK)
        dt = _softplus(dt_ref[sl, :] + dtb)
        cum = _dot_exact(tri_f, dt * a_row)
        cum_x = _dot_exact(cum, expand)
        dt_x = _dot_exact(dt, expand)
        cum2 = jnp.concatenate([cum, cum], axis=0)
        cum2_t = cum2.T
        xc = xc_ref[sl, :]
        xdt = xc * dt_x
        last_x = cum_x[CHUNK - 1:CHUNK, :]
        xd_end = (xdt * jnp.exp(last_x - cum_x)).astype(BF16)
        ecum_x = jnp.exp(cum_x)
        dec_x = jnp.exp(last_x)
        xdt_b = xdt.astype(BF16)
        ys = []
        for g in range(SSD_GROUPS):
            gs = slice(g * SSD_GROUP_WIDTH, (g + 1) * SSD_GROUP_WIDTH)
            bm = bcc_ref[sl, g * SSD_STATE:(g + 1) * SSD_STATE].astype(BF16)
            cm = bcc_ref[sl, (SSD_GROUPS + g) * SSD_STATE:
                         (SSD_GROUPS + g + 1) * SSD_STATE].astype(BF16)
            cb2 = _dot_nt(cm, jnp.concatenate([bm, bm], axis=0))
            st = st_ref[g]
            y_off = _dot(cm, st.astype(BF16)) * ecum_x[:, gs]
            st_ref[g] = st * dec_x[:, gs] + _dot_tn(bm, xd_end[:, gs])
            pieces = []
            for p in range(SSD_GROUP_WIDTH // LANES):
                h0 = g * (SSD_HEADS // SSD_GROUPS) + 2 * p
                cs = slice(g * SSD_GROUP_WIDTH + p * LANES, g * SSD_GROUP_WIDTH + (p + 1) * LANES)
                col = cum_x[:, cs]
                row = jnp.where(left, cum2_t[h0:h0 + 1, :], cum2_t[h0 + 1:h0 + 2, :])
                decay = jnp.exp(jnp.where(tri2, col - row, -jnp.inf))
                m2 = (cb2 * decay).astype(BF16)
                xp = xdt_b[:, cs]
                zero = jnp.zeros_like(xp)
                x2 = jnp.concatenate([jnp.where(left, xp, zero), jnp.where(left, zero, xp)],
                                     axis=0)
                pieces.append(_dot(m2, x2))
            ys.append(jnp.concatenate(pieces, axis=1) + y_off)
        y = jnp.concatenate(ys, axis=1) + dx_ref[...] * xc
        y = y * _silu(z_ref[sl, :].astype(F32))
        nw = nw_ref[...]
        for g in range(SSD_GROUPS):
            gs = slice(g * SSD_GROUP_WIDTH, (g + 1) * SSD_GROUP_WIDTH)
            yg = y[:, gs]
            ms = jnp.mean(yg * yg, axis=-1, keepdims=True)
            o_ref[sl, gs] = (yg * lax.rsqrt(ms + EPS) * nw[:, gs]).astype(o_ref.dtype)
        return carry

    lax.fori_loop(0, ts // CHUNK, chunk, 0)


def _ssd(proj, dt_raw, conv_wx, conv_wb, conv_bx, conv_bb, dt_bias, a_log, d_x, norm_w,
         batch, seq, ts):
    n = proj.shape[0]
    nt = seq // ts
    width = SSD_GROUPS * SSD_GROUP_WIDTH
    bcw = 2 * SSD_GROUPS * SSD_STATE

    def full(shape):
        return pl.BlockSpec(shape, lambda b, t: (0, 0))

    return pl.pallas_call(
        functools.partial(_ssd_body, ts=ts),
        grid=(batch, nt),
        in_specs=[
            pl.BlockSpec((ts, width), lambda b, t: (b * nt + t, 4)),
            pl.BlockSpec((ts, width), lambda b, t: (b * nt + t, 5)),
            pl.BlockSpec((ts, bcw), lambda b, t: (b * nt + t, 12)),
            pl.BlockSpec((ts, LANES), lambda b, t: (b * nt + t, 0)),
            full((SSD_CONV, width)), full((SSD_CONV, bcw)),
            full((1, width)), full((1, bcw)),
            full((1, LANES)), full((1, LANES)),
            full((1, width)), full((1, width)),
        ],
        out_specs=pl.BlockSpec((ts, width), lambda b, t: (b * nt + t, 0)),
        out_shape=jax.ShapeDtypeStruct((n, width), BF16),
        scratch_shapes=[
            pltpu.VMEM((ts + 8, width), F32),
            pltpu.VMEM((ts + 8, bcw), F32),
            pltpu.VMEM((ts, width), F32),
            pltpu.VMEM((ts, bcw), F32),
            pltpu.VMEM((SSD_GROUPS, SSD_STATE, SSD_GROUP_WIDTH), F32),
        ],
        compiler_params=pltpu.CompilerParams(
            dimension_semantics=("arbitrary", "arbitrary"),
            vmem_limit_bytes=VMEM_LIMIT_BYTES),
        name="ssd",
    )(proj, proj, proj, dt_raw, conv_wx, conv_wb, conv_bx, conv_bb, dt_bias, a_log, d_x, norm_w)


def _outproj_body(x_ref, hg_ref, ssd_ref, w1_ref, w2_ref, nw_ref, x2_ref, xn_ref):
    x2 = x_ref[...] + _dot(hg_ref[...], w1_ref[...]) + _dot(ssd_ref[...], w2_ref[...])
    x2_ref[...] = x2
    ms = jnp.mean(x2 * x2, axis=-1, keepdims=True)
    xn_ref[...] = (x2 * lax.rsqrt(ms + EPS) * nw_ref[...]).astype(BF16)


def _outproj(x2d, o_hg, o_ssd, w1, w2, norm_w, tm):
    n = x2d.shape[0]
    row = pl.BlockSpec((tm, D_MODEL), lambda i: (i, 0))
    wfull = pl.BlockSpec((D_MODEL, D_MODEL), lambda i: (0, 0))
    return pl.pallas_call(
        _outproj_body,
        grid=(n // tm,),
        in_specs=[row, row, row, wfull, wfull, pl.BlockSpec((1, D_MODEL), lambda i: (0, 0))],
        out_specs=[row, row],
        out_shape=[jax.ShapeDtypeStruct((n, D_MODEL), F32),
                   jax.ShapeDtypeStruct((n, D_MODEL), BF16)],
        compiler_params=pltpu.CompilerParams(
            dimension_semantics=("arbitrary",), vmem_limit_bytes=VMEM_LIMIT_BYTES),
        name="outproj",
    )(x2d, o_hg, o_ssd, w1, w2, norm_w)


def _top_values(s, k):
    vals = []
    for _ in range(k):
        mx = jnp.max(s, axis=0, keepdims=True)
        vals.append(mx)
        s = jnp.where(s >= mx, -jnp.inf, s)
    return vals


def _candidate_sums(a, b):
    a16 = jnp.concatenate(a, axis=0)
    b16 = jnp.concatenate(b, axis=0)
    b8 = b16[0:8]
    rank = lax.broadcasted_iota(jnp.int32, b8.shape, 0) + 1
    rows = [a[0] + b16, a[1] + b8]
    for k in range(3, 9):
        rows.append(jnp.where(rank * k <= PEER_TOPK, a[k - 1] + b8, -jnp.inf))
    rows.append(a16[8:16] + b[0])
    return jnp.concatenate(rows, axis=0)


def _top_values_ranked(s, k):
    vals = []
    rank = jnp.full(s.shape, 2.0 * k, F32)
    for r in range(k):
        mx = jnp.max(s, axis=0, keepdims=True)
        vals.append(mx)
        hit = s >= mx
        rank = jnp.where(hit, r + 1.0, rank)
        s = jnp.where(hit, -jnp.inf, s)
    return vals, rank


def _pair_of_bf16(x):
    bits = lax.bitcast_convert_type(x.astype(BF16).astype(F32), jnp.uint32) >> 16
    return bits | (bits << 16)


def _packed_row(tile, s):
    return pltpu.bitcast(jnp.broadcast_to(tile[s:s + 1, :], (8, LANES)), BF16)


def _peer_body(xn_ref, x2_ref, wq_ref, keys_ref, u_ref, vt_ref, fw_ref, o_ref,
               s1_ref, s2_ref, n_ref, c1_ref, r2_ref, e2_ref, acc_ref, *, tt, eb, eq):
    e = pl.program_id(1)
    ib = eb // PEER_KEYS
    n_tc = tt // LANES
    pk = 16

    @pl.when(e == 0)
    def _():
        q = _dot(xn_ref[...], wq_ref[...]).astype(BF16)
        for h in range(PEER_HEADS):
            for p, ref in ((0, s1_ref), (1, s2_ref)):
                c0 = (2 * h + p) * PEER_KEYS
                s = _dot_nt(keys_ref[2 * h + p], q[:, c0:c0 + PEER_KEYS])
                for tc in range(n_tc):
                    ref[h * n_tc + tc] = s[:, tc * LANES:(tc + 1) * LANES]

        def stats(k, carry):
            s1 = s1_ref[k]
            s2 = s2_ref[k]
            a = _top_values(s1, PEER_TOPK)
            b, rank2 = _top_values_ranked(s2, PEER_TOPK)
            cand = _candidate_sums(a, b)
            tau = _top_values(cand, PEER_TOPK)[-1]
            m = a[0] + b[0]
            z = jnp.sum(jnp.where(cand >= tau, jnp.exp(cand - m), 0.0), axis=0, keepdims=True)
            count = jnp.zeros_like(s1)
            for l in range(PEER_TOPK):
                count = jnp.where(s1 + b[l] >= tau, l + 1.0, count)
            n_ref[k] = _pair_of_bf16(count)
            c1_ref[k] = _pair_of_bf16(jnp.exp(s1 - a[0]))
            r2_ref[k] = rank2.astype(BF16)
            e2_ref[k] = (jnp.exp(s2 - b[0]) / z).astype(BF16)
            return carry

        lax.fori_loop(0, PEER_HEADS * n_tc, stats, 0)
        acc_ref[...] = jnp.zeros_like(acc_ref)

    xn = xn_ref[...]

    def pre_activations(r0):
        return _dot_nt(u_ref[r0:r0 + eq, :], xn)

    hq_next = pre_activations(0)
    at_prev = None
    for r0 in range(0, eb, eq):
        hq = hq_next
        if r0 + eq < eb:
            hq_next = pre_activations(r0 + eq)
        if at_prev is not None:
            acc_ref[...] += _dot(vt_ref[0, :, r0 - eq:r0], at_prev)
        n_ii = eq // PEER_KEYS
        n_jv = PEER_KEYS // pk
        zero = jnp.zeros((pk, LANES), BF16)
        blocks = [[None] * n_tc for _ in range(n_ii)]
        for tc in range(n_tc):
            w = [[None] * n_jv for _ in range(n_ii)]
            for h in range(PEER_HEADS):
                k = h * n_tc + tc
                rank2 = [r2_ref[k, jv * pk:(jv + 1) * pk, :] for jv in range(n_jv)]
                e2 = [e2_ref[k, jv * pk:(jv + 1) * pk, :] for jv in range(n_jv)]
                for ii in range(n_ii):
                    s = r0 // PEER_KEYS + ii
                    rows8 = pl.ds(pl.multiple_of(e * ib + (s // 8) * 8, 8), 8)
                    count = _packed_row(n_ref[k, rows8, :], s % 8)
                    c1 = _packed_row(c1_ref[k, rows8, :], s % 8)
                    for jv in range(n_jv):
                        g = jnp.where(rank2[jv] <= count, e2[jv], zero) * c1
                        w[ii][jv] = g if h == 0 else w[ii][jv] + g
            for ii in range(n_ii):
                hh = hq[ii * PEER_KEYS:(ii + 1) * PEER_KEYS, tc * LANES:(tc + 1) * LANES]
                blocks[ii][tc] = _gelu(hh).astype(BF16) * jnp.concatenate(w[ii], axis=0)
        at_prev = jnp.concatenate([jnp.concatenate(row, axis=1) for row in blocks],
                                  axis=0)
    acc_ref[...] += _dot(vt_ref[0, :, eb - eq:eb], at_prev)

    @pl.when(e == pl.num_programs(1) - 1)
    def _():
        y = x2_ref[...] + acc_ref[...].T
        ms = jnp.mean(y * y, axis=-1, keepdims=True)
        o_ref[...] = y * lax.rsqrt(ms + EPS) * fw_ref[...]


def _peer(xn, x2, wq, keys, u, vt, final_w, tt, eb, eq):
    n = xn.shape[0]
    n_exp = u.shape[0]
    tok = lambda i, e: (i, 0)
    const2 = lambda i, e: (0, 0)
    def stat(dtype):
        return pltpu.VMEM((PEER_HEADS * (tt // LANES), PEER_KEYS, LANES), dtype)

    return pl.pallas_call(
        functools.partial(_peer_body, tt=tt, eb=eb, eq=eq),
        grid=(n // tt, n_exp // eb),
        in_specs=[
            pl.BlockSpec((tt, D_MODEL), tok),
            pl.BlockSpec((tt, D_MODEL), tok),
            pl.BlockSpec(wq.shape, const2),
            pl.BlockSpec(keys.shape, lambda i, e: (0, 0, 0)),
            pl.BlockSpec((eb, D_MODEL), lambda i, e: (e, 0)),
            pl.BlockSpec((1, D_MODEL, eb), lambda i, e: (e, 0, 0)),
            pl.BlockSpec((1, D_MODEL), const2),
        ],
        out_specs=pl.BlockSpec((tt, D_MODEL), tok),
        out_shape=jax.ShapeDtypeStruct((n, D_MODEL), F32),
        scratch_shapes=[stat(F32), stat(F32), stat(jnp.uint32), stat(jnp.uint32),
                        stat(BF16), stat(BF16), pltpu.VMEM((D_MODEL, tt), F32)],
        compiler_params=pltpu.CompilerParams(
            dimension_semantics=("arbitrary", "arbitrary"),
            vmem_limit_bytes=VMEM_LIMIT_BYTES),
        name="peer",
    )(xn, x2, wq, keys, u, vt, final_w)


def _pad_lanes(v):
    return jnp.pad(v, (0, LANES - v.shape[0])).reshape(1, LANES)


def _tile(n, want):
    return want if n % want == 0 else n


def kernel(x, norm_mix_w, w_in, hg_lb_logits, hg_norm_w, ssd_conv_w, ssd_conv_b,
           ssd_dt_bias, ssd_A_log, ssd_D, ssd_norm_w, w_out, norm_ffn_w,
           peer_w_q, peer_keys, peer_u, peer_v, final_norm_w):
    batch, seq, d = x.shape
    n = batch * seq
    x2d = x.reshape(n, d)
    width = SSD_GROUPS * SSD_GROUP_WIDTH

    w_main = w_in[0][:, :PROJ_MAIN].astype(BF16)
    w_dt = jnp.pad(w_in[0][:, PROJ_MAIN:], ((0, 0), (0, LANES - SSD_HEADS))).astype(BF16)
    conv_w = ssd_conv_w[0]
    conv_b = ssd_conv_b[0].reshape(1, -1)
    w_out1 = w_out[0][:HG_HEADS * HG_HEAD_DIM].astype(BF16)
    w_out2 = w_out[0][HG_HEADS * HG_HEAD_DIM:].astype(BF16)
    keys = peer_keys[0].reshape(PEER_HEADS * 2, PEER_KEYS, -1).astype(BF16)
    u = peer_u[0].astype(BF16)
    vt = peer_v[0].astype(BF16).reshape(-1, 1024, D_MODEL).transpose(0, 2, 1)

    proj, dt_raw = _inproj(x2d, norm_mix_w[0].reshape(1, d), w_main, w_dt,
                           _tile(n, 1024), PROJ_MAIN // 4)
    o_hg = _hgrn2(proj, hg_lb_logits, hg_norm_w[0].reshape(1, -1), batch, seq, _tile(seq, 512))
    o_ssd = _ssd(proj, dt_raw, conv_w[:, :width], conv_w[:, width:], conv_b[:, :width],
                 conv_b[:, width:], _pad_lanes(ssd_dt_bias[0]), _pad_lanes(ssd_A_log[0]),
                 jnp.repeat(ssd_D[0], SSD_HEAD_DIM).reshape(1, width),
                 ssd_norm_w[0].reshape(1, width), batch, seq, _tile(seq, 256))
    x2, xn = _outproj(x2d, o_hg, o_ssd, w_out1, w_out2, norm_ffn_w[0].reshape(1, d),
                      _tile(n, 512))
    out = _peer(xn, x2, peer_w_q[0].astype(BF16), keys, u, vt, final_norm_w.reshape(1, d),
                _tile(n, 512), 1024, 256)
    return out.reshape(batch, seq, d)
```

```python
import functools
import math

import jax
import jax.numpy as jnp
from jax import lax
from jax.experimental import pallas as pl
from jax.experimental.pallas import tpu as pltpu

F32 = jnp.float32
BF16 = jnp.bfloat16
EPS = 1e-6
HIGHEST = lax.Precision.HIGHEST

D_MODEL = 1024
HG_HEADS = 8
HG_HEAD_DIM = 128
CHUNK = 64
SSD_HEADS = 16
SSD_HEAD_DIM = 64
SSD_GROUPS = 2
SSD_STATE = 128
SSD_CONV = 4
SSD_GROUP_WIDTH = 512
PEER_HEADS = 8
PEER_KEYS = 128
PEER_TOPK = 16
PROJ_MAIN = 6656
LANES = 128

VMEM_LIMIT_BYTES = 56 * 1024 * 1024


def _dot(a, b):
    return jnp.dot(a, b, preferred_element_type=F32)


def _dot_nt(a, b):
    return lax.dot_general(a, b, (((1,), (1,)), ((), ())), preferred_element_type=F32)


def _dot_tn(a, b):
    return lax.dot_general(a, b, (((0,), (0,)), ((), ())), preferred_element_type=F32)


def _split3(x):
    hi = x.astype(BF16)
    r1 = x - hi.astype(F32)
    mid = r1.astype(BF16)
    lo = (r1 - mid.astype(F32)).astype(BF16)
    return hi, mid, lo


def _dot_exact(a, b, exact_side):
    if exact_side == "a":
        a16 = a.astype(BF16)
        return sum(_dot(a16, piece) for piece in _split3(b))
    b16 = b.astype(BF16)
    return sum(_dot(piece, b16) for piece in _split3(a))


def _sigmoid(x):
    return 1.0 / (1.0 + jnp.exp(-x))


def _silu(x):
    return x * _sigmoid(x)


def _softplus(x):
    return jnp.maximum(x, 0.0) + jnp.log(1.0 + jnp.exp(-jnp.abs(x)))


def _gelu(x):
    return 0.5 * x * (1.0 + lax.erf(x * (1.0 / math.sqrt(2.0))))


def _tri(n, m=None):
    m = n if m is None else m
    r = lax.broadcasted_iota(jnp.int32, (n, m), 0)
    c = lax.broadcasted_iota(jnp.int32, (n, m), 1)
    return r >= (c % n)


def _inproj_body(x_ref, nw_ref, w_ref, wdt_ref, o_ref, dt_ref, hn_ref):
    @pl.when(pl.program_id(1) == 0)
    def _():
        x = x_ref[...]
        ms = jnp.mean(x * x, axis=-1, keepdims=True)
        hn = (x * lax.rsqrt(ms + EPS) * nw_ref[...]).astype(BF16)
        hn_ref[...] = hn
        dt_ref[...] = _dot(hn, wdt_ref[...])

    o_ref[...] = _dot(hn_ref[...], w_ref[...]).astype(o_ref.dtype)


def _inproj(x2d, norm_w, w_main, w_dt, tm, tn):
    n = x2d.shape[0]
    return pl.pallas_call(
        _inproj_body,
        grid=(n // tm, PROJ_MAIN // tn),
        in_specs=[
            pl.BlockSpec((tm, D_MODEL), lambda i, j: (i, 0)),
            pl.BlockSpec((1, D_MODEL), lambda i, j: (0, 0)),
            pl.BlockSpec((D_MODEL, tn), lambda i, j: (0, j)),
            pl.BlockSpec((D_MODEL, LANES), lambda i, j: (0, 0)),
        ],
        out_specs=[
            pl.BlockSpec((tm, tn), lambda i, j: (i, j)),
            pl.BlockSpec((tm, LANES), lambda i, j: (i, 0)),
        ],
        out_shape=[
            jax.ShapeDtypeStruct((n, PROJ_MAIN), BF16),
            jax.ShapeDtypeStruct((n, LANES), F32),
        ],
        scratch_shapes=[pltpu.VMEM((tm, D_MODEL), BF16)],
        compiler_params=pltpu.CompilerParams(
            dimension_semantics=("arbitrary", "arbitrary"),
            vmem_limit_bytes=VMEM_LIMIT_BYTES),
        name="inproj",
    )(x2d, norm_w, w_main, w_dt)


def _hgrn2_body(q_ref, f_ref, i_ref, g_ref, lbl_ref, nw_ref, o_ref, st_ref, *, n_chunks):
    @pl.when(pl.program_id(1) == 0)
    def _():
        st_ref[...] = jnp.zeros_like(st_ref)

    lbl = lbl_ref[...]
    e = jnp.exp(lbl - jnp.max(lbl, axis=0, keepdims=True))
    lb = e[0:1, :] / jnp.sum(e, axis=0, keepdims=True)
    tri = _tri(CHUNK)
    tri_f = tri.astype(F32)
    nw = nw_ref[...]
    hd = HG_HEAD_DIM

    def chunk(c, carry):
        sl = pl.ds(pl.multiple_of(c * CHUNK, CHUNK), CHUNK)
        f = lb + (1.0 - lb) * _sigmoid(f_ref[sl, :].astype(F32))
        k = 1.0 - f
        g_cum = _dot_exact(tri_f, jnp.log(f), "a")
        g_last = g_cum[CHUNK - 1:CHUNK, :]
        q_dec = (q_ref[sl, :].astype(F32) * jnp.exp(g_cum)).astype(BF16)
        k_dec = (k * jnp.exp(-g_cum)).astype(BF16)
        k_end = (k * jnp.exp(g_last - g_cum)).astype(BF16)
        decay = jnp.exp(g_last)
        v = i_ref[sl, :]
        gate = _silu(g_ref[sl, :].astype(F32))
        for h in range(HG_HEADS):
            hs = slice(h * hd, (h + 1) * hd)
            scores = jnp.where(tri, _dot_nt(q_dec[:, hs], k_dec[:, hs]), 0.0)
            st = st_ref[h]
            o = _dot(scores.astype(BF16), v[:, hs]) + _dot_nt(q_dec[:, hs], st.astype(BF16))
            st_ref[h] = st * decay[:, hs] + _dot_tn(v[:, hs], k_end[:, hs])
            ms = jnp.mean(o * o, axis=-1, keepdims=True)
            o_ref[sl, hs] = (o * lax.rsqrt(ms + EPS) * nw * gate[:, hs]).astype(o_ref.dtype)
        return carry

    lax.fori_loop(0, n_chunks, chunk, 0)


def _hgrn2(proj, lb_logits, norm_w, batch, seq, ts):
    n = proj.shape[0]
    nt = seq // ts
    hd = HG_HEAD_DIM
    width = HG_HEADS * hd

    def col(k):
        return pl.BlockSpec((ts, width), lambda b, t: (b * nt + t, k))

    return pl.pallas_call(
        functools.partial(_hgrn2_body, n_chunks=ts // CHUNK),
        grid=(batch, nt),
        in_specs=[
            col(0), col(1), col(2), col(3),
            pl.BlockSpec((2, width), lambda b, t: (0, 0)),
            pl.BlockSpec((1, hd), lambda b, t: (0, 0)),
        ],
        out_specs=pl.BlockSpec((ts, width), lambda b, t: (b * nt + t, 0)),
        out_shape=jax.ShapeDtypeStruct((n, width), BF16),
        scratch_shapes=[pltpu.VMEM((HG_HEADS, hd, hd), F32)],
        compiler_params=pltpu.CompilerParams(
            dimension_semantics=("arbitrary", "arbitrary"),
            vmem_limit_bytes=VMEM_LIMIT_BYTES),
        name="hgrn2",
    )(proj, proj, proj, proj, lb_logits, norm_w)


def _ssd_body(z_ref, xs_ref, bc_ref, dt_ref, cwx_ref, cwb_ref, cbx_ref, cbb_ref,
              dtb_ref, alog_ref, dx_ref, nw_ref, o_ref,
              xpad_ref, bpad_ref, xc_ref, bcc_ref, st_ref, *, ts):
    width = SSD_GROUPS * SSD_GROUP_WIDTH
    bcw = 2 * SSD_GROUPS * SSD_STATE

    @pl.when(pl.program_id(1) == 0)
    def _():
        st_ref[...] = jnp.zeros_like(st_ref)
        xpad_ref[0:8, :] = jnp.zeros((8, width), F32)
        bpad_ref[0:8, :] = jnp.zeros((8, bcw), F32)

    xpad_ref[8:, :] = xs_ref[...].astype(F32)
    bpad_ref[8:, :] = bc_ref[...].astype(F32)
    accx = cbx_ref[...]
    accb = cbb_ref[...]
    for j in range(SSD_CONV):
        lo = 8 - (SSD_CONV - 1) + j
        accx = accx + xpad_ref[lo:lo + ts, :] * cwx_ref[j:j + 1, :]
        accb = accb + bpad_ref[lo:lo + ts, :] * cwb_ref[j:j + 1, :]
    xc_ref[...] = _silu(accx)
    bcc_ref[...] = _silu(accb)
    xpad_ref[0:8, :] = xpad_ref[ts:ts + 8, :]
    bpad_ref[0:8, :] = bpad_ref[ts:ts + 8, :]

    a_row = -jnp.exp(alog_ref[...])
    dtb = dtb_ref[...]
    tri_f = _tri(CHUNK).astype(F32)
    tri2 = _tri(CHUNK, 2 * CHUNK)
    lane = lax.broadcasted_iota(jnp.int32, (1, LANES), 1)
    left = lane < SSD_HEAD_DIM
    hrow = lax.broadcasted_iota(jnp.int32, (LANES, width), 0)
    hcol = lax.broadcasted_iota(jnp.int32, (LANES, width), 1) // SSD_HEAD_DIM
    expand = (hrow == hcol).astype(F32)

    def chunk(c, carry):
        sl = pl.ds(pl.multiple_of(c * CHUNK, CHUNK), CHUNK)
        dt = _softplus(dt_ref[sl, :] + dtb)
        cum = _dot_exact(tri_f, dt * a_row, "a")
        cum_x = _dot_exact(cum, expand, "b")
        dt_x = _dot_exact(dt, expand, "b")
        cum2 = jnp.concatenate([cum, cum], axis=0)
        cum2_t = cum2.T
        xc = xc_ref[sl, :]
        xdt = xc * dt_x
        last_x = cum_x[CHUNK - 1:CHUNK, :]
        xd_end = (xdt * jnp.exp(last_x - cum_x)).astype(BF16)
        ecum_x = jnp.exp(cum_x)
        dec_x = jnp.exp(last_x)
        xdt_b = xdt.astype(BF16)
        ys = []
        for g in range(SSD_GROUPS):
            gs = slice(g * SSD_GROUP_WIDTH, (g + 1) * SSD_GROUP_WIDTH)
            bm = bcc_ref[sl, g * SSD_STATE:(g + 1) * SSD_STATE].astype(BF16)
            cm = bcc_ref[sl, (SSD_GROUPS + g) * SSD_STATE:
                         (SSD_GROUPS + g + 1) * SSD_STATE].astype(BF16)
            cb2 = _dot_nt(cm, jnp.concatenate([bm, bm], axis=0))
            st = st_ref[g]
            y_off = _dot(cm, st.astype(BF16)) * ecum_x[:, gs]
            st_ref[g] = st * dec_x[:, gs] + _dot_tn(bm, xd_end[:, gs])
            pieces = []
            for p in range(SSD_GROUP_WIDTH // LANES):
                h0 = g * (SSD_HEADS // SSD_GROUPS) + 2 * p
                cs = slice(g * SSD_GROUP_WIDTH + p * LANES, g * SSD_GROUP_WIDTH + (p + 1) * LANES)
                col = cum_x[:, cs]
                row = jnp.where(left, cum2_t[h0:h0 + 1, :], cum2_t[h0 + 1:h0 + 2, :])
                decay = jnp.exp(jnp.where(tri2, col - row, -jnp.inf))
                m2 = (cb2 * decay).astype(BF16)
                xp = xdt_b[:, cs]
                zero = jnp.zeros_like(xp)
                x2 = jnp.concatenate([jnp.where(left, xp, zero), jnp.where(left, zero, xp)],
                                     axis=0)
                pieces.append(_dot(m2, x2))
            ys.append(jnp.concatenate(pieces, axis=1) + y_off)
        y = jnp.concatenate(ys, axis=1) + dx_ref[...] * xc
        y = y * _silu(z_ref[sl, :].astype(F32))
        nw = nw_ref[...]
        for g in range(SSD_GROUPS):
            gs = slice(g * SSD_GROUP_WIDTH, (g + 1) * SSD_GROUP_WIDTH)
            yg = y[:, gs]
            ms = jnp.mean(yg * yg, axis=-1, keepdims=True)
            o_ref[sl, gs] = (yg * lax.rsqrt(ms + EPS) * nw[:, gs]).astype(o_ref.dtype)
        return carry

    lax.fori_loop(0, ts // CHUNK, chunk, 0, unroll=2)


def _ssd(proj, dt_raw, conv_wx, conv_wb, conv_bx, conv_bb, dt_bias, a_log, d_x, norm_w,
         batch, seq, ts):
    n = proj.shape[0]
    nt = seq // ts
    width = SSD_GROUPS * SSD_GROUP_WIDTH
    bcw = 2 * SSD_GROUPS * SSD_STATE

    def full(shape):
        return pl.BlockSpec(shape, lambda b, t: (0, 0))

    return pl.pallas_call(
        functools.partial(_ssd_body, ts=ts),
        grid=(batch, nt),
        in_specs=[
            pl.BlockSpec((ts, width), lambda b, t: (b * nt + t, 4)),
            pl.BlockSpec((ts, width), lambda b, t: (b * nt + t, 5)),
            pl.BlockSpec((ts, bcw), lambda b, t: (b * nt + t, 12)),
            pl.BlockSpec((ts, LANES), lambda b, t: (b * nt + t, 0)),
            full((SSD_CONV, width)), full((SSD_CONV, bcw)),
            full((1, width)), full((1, bcw)),
            full((1, LANES)), full((1, LANES)),
            full((1, width)), full((1, width)),
        ],
        out_specs=pl.BlockSpec((ts, width), lambda b, t: (b * nt + t, 0)),
        out_shape=jax.ShapeDtypeStruct((n, width), BF16),
        scratch_shapes=[
            pltpu.VMEM((ts + 8, width), F32),
            pltpu.VMEM((ts + 8, bcw), F32),
            pltpu.VMEM((ts, width), F32),
            pltpu.VMEM((ts, bcw), F32),
            pltpu.VMEM((SSD_GROUPS, SSD_STATE, SSD_GROUP_WIDTH), F32),
        ],
        compiler_params=pltpu.CompilerParams(
            dimension_semantics=("arbitrary", "arbitrary"),
            vmem_limit_bytes=VMEM_LIMIT_BYTES),
        name="ssd",
    )(proj, proj, proj, dt_raw, conv_wx, conv_wb, conv_bx, conv_bb, dt_bias, a_log, d_x, norm_w)


def _outproj_body(x_ref, hg_ref, ssd_ref, w1_ref, w2_ref, nw_ref, x2_ref, xn_ref):
    x2 = x_ref[...] + _dot(hg_ref[...], w1_ref[...]) + _dot(ssd_ref[...], w2_ref[...])
    x2_ref[...] = x2
    ms = jnp.mean(x2 * x2, axis=-1, keepdims=True)
    xn_ref[...] = (x2 * lax.rsqrt(ms + EPS) * nw_ref[...]).astype(BF16)


def _outproj(x2d, o_hg, o_ssd, w1, w2, norm_w, tm):
    n = x2d.shape[0]
    row = pl.BlockSpec((tm, D_MODEL), lambda i: (i, 0))
    wfull = pl.BlockSpec((D_MODEL, D_MODEL), lambda i: (0, 0))
    return pl.pallas_call(
        _outproj_body,
        grid=(n // tm,),
        in_specs=[row, row, row, wfull, wfull, pl.BlockSpec((1, D_MODEL), lambda i: (0, 0))],
        out_specs=[row, row],
        out_shape=[jax.ShapeDtypeStruct((n, D_MODEL), F32),
                   jax.ShapeDtypeStruct((n, D_MODEL), BF16)],
        compiler_params=pltpu.CompilerParams(
            dimension_semantics=("arbitrary",), vmem_limit_bytes=VMEM_LIMIT_BYTES),
        name="outproj",
    )(x2d, o_hg, o_ssd, w1, w2, norm_w)


def _top_values(s, k):
    vals = []
    for _ in range(k):
        mx = jnp.max(s, axis=0, keepdims=True)
        vals.append(mx)
        s = jnp.where(s >= mx, -jnp.inf, s)
    return vals


def _candidate_sums(a, b):
    a16 = jnp.concatenate(a, axis=0)
    b16 = jnp.concatenate(b, axis=0)
    b8 = b16[0:8]
    rank = lax.broadcasted_iota(jnp.int32, b8.shape, 0) + 1
    rows = [a[0] + b16, a[1] + b8]
    for k in range(3, 9):
        rows.append(jnp.where(rank * k <= PEER_TOPK, a[k - 1] + b8, -jnp.inf))
    rows.append(a16[8:16] + b[0])
    return jnp.concatenate(rows, axis=0)


def _top_values_ranked(s, k):
    vals = []
    rank = jnp.full(s.shape, 2.0 * k, F32)
    for r in range(k):
        mx = jnp.max(s, axis=0, keepdims=True)
        vals.append(mx)
        hit = s >= mx
        rank = jnp.where(hit, r + 1.0, rank)
        s = jnp.where(hit, -jnp.inf, s)
    return vals, rank


def _pair_of_bf16(x):
    bits = lax.bitcast_convert_type(x.astype(BF16).astype(F32), jnp.uint32) >> 16
    return bits | (bits << 16)


def _packed_row(tile, s):
    return pltpu.bitcast(jnp.broadcast_to(tile[s:s + 1, :], (8, LANES)), BF16)


def _peer_body(xn_ref, x2_ref, wq_ref, keys_ref, u_ref, vt_ref, fw_ref, o_ref,
               s1_ref, s2_ref, n_ref, c1_ref, r2_ref, e2_ref, acc_ref, *, tt, eb, eq):
    e = pl.program_id(1)
    ib = eb // PEER_KEYS
    n_tc = tt // LANES
    pk = 16

    @pl.when(e == 0)
    def _():
        q = _dot(xn_ref[...], wq_ref[...]).astype(BF16)
        for h in range(PEER_HEADS):
            for p, ref in ((0, s1_ref), (1, s2_ref)):
                c0 = (2 * h + p) * PEER_KEYS
                s = _dot_nt(keys_ref[2 * h + p], q[:, c0:c0 + PEER_KEYS])
                for tc in range(n_tc):
                    ref[h * n_tc + tc] = s[:, tc * LANES:(tc + 1) * LANES]

        def stats(k, carry):
            s1 = s1_ref[k]
            s2 = s2_ref[k]
            a = _top_values(s1, PEER_TOPK)
            b, rank2 = _top_values_ranked(s2, PEER_TOPK)
            cand = _candidate_sums(a, b)
            tau = _top_values(cand, PEER_TOPK)[-1]
            m = a[0] + b[0]
            z = jnp.sum(jnp.where(cand >= tau, jnp.exp(cand - m), 0.0), axis=0, keepdims=True)
            count = jnp.zeros_like(s1)
            for l in range(PEER_TOPK):
                count = jnp.where(s1 + b[l] >= tau, l + 1.0, count)
            n_ref[k] = _pair_of_bf16(count)
            c1_ref[k] = _pair_of_bf16(jnp.exp(s1 - a[0]))
            r2_ref[k] = rank2.astype(BF16)
            e2_ref[k] = (jnp.exp(s2 - b[0]) / z).astype(BF16)
            return carry

        lax.fori_loop(0, PEER_HEADS * n_tc, stats, 0)
        acc_ref[...] = jnp.zeros_like(acc_ref)

    xn = xn_ref[...]

    def pre_activations(r0):
        return _dot_nt(u_ref[r0:r0 + eq, :], xn)

    hq_next = pre_activations(0)
    at_prev = None
    for r0 in range(0, eb, eq):
        hq = hq_next
        if r0 + eq < eb:
            hq_next = pre_activations(r0 + eq)
        if at_prev is not None:
            acc_ref[...] += _dot(vt_ref[0, :, r0 - eq:r0], at_prev)
        n_ii = eq // PEER_KEYS
        n_jv = PEER_KEYS // pk
        zero = jnp.zeros((pk, LANES), BF16)
        blocks = [[None] * n_tc for _ in range(n_ii)]
        for tc in range(n_tc):
            w = [[None] * n_jv for _ in range(n_ii)]
            for h in range(PEER_HEADS):
                k = h * n_tc + tc
                rank2 = [r2_ref[k, jv * pk:(jv + 1) * pk, :] for jv in range(n_jv)]
                e2 = [e2_ref[k, jv * pk:(jv + 1) * pk, :] for jv in range(n_jv)]
                for ii in range(n_ii):
                    s = r0 // PEER_KEYS + ii
                    rows8 = pl.ds(pl.multiple_of(e * ib + (s // 8) * 8, 8), 8)
                    count = _packed_row(n_ref[k, rows8, :], s % 8)
                    c1 = _packed_row(c1_ref[k, rows8, :], s % 8)
                    for jv in range(n_jv):
                        g = jnp.where(rank2[jv] <= count, e2[jv], zero) * c1
                        w[ii][jv] = g if h == 0 else w[ii][jv] + g
            for ii in range(n_ii):
                hh = hq[ii * PEER_KEYS:(ii + 1) * PEER_KEYS, tc * LANES:(tc + 1) * LANES]
                blocks[ii][tc] = _gelu(hh).astype(BF16) * jnp.concatenate(w[ii], axis=0)
        at_prev = jnp.concatenate([jnp.concatenate(row, axis=1) for row in blocks],
                                  axis=0)
    acc_ref[...] += _dot(vt_ref[0, :, eb - eq:eb], at_prev)

    @pl.when(e == pl.num_programs(1) - 1)
    def _():
        y = x2_ref[...] + acc_ref[...].T
        ms = jnp.mean(y * y, axis=-1, keepdims=True)
        o_ref[...] = y * lax.rsqrt(ms + EPS) * fw_ref[...]


def _peer(xn, x2, wq, keys, u, vt, final_w, tt, eb, eq):
    n = xn.shape[0]
    n_exp = u.shape[0]
    tok = lambda i, e: (i, 0)
    const2 = lambda i, e: (0, 0)
    def stat(dtype):
        return pltpu.VMEM((PEER_HEADS * (tt // LANES), PEER_KEYS, LANES), dtype)

    return pl.pallas_call(
        functools.partial(_peer_body, tt=tt, eb=eb, eq=eq),
        grid=(n // tt, n_exp // eb),
        in_specs=[
            pl.BlockSpec((tt, D_MODEL), tok),
            pl.BlockSpec((tt, D_MODEL), tok),
            pl.BlockSpec(wq.shape, const2),
            pl.BlockSpec(keys.shape, lambda i, e: (0, 0, 0)),
            pl.BlockSpec((eb, D_MODEL), lambda i, e: (e, 0)),
            pl.BlockSpec((1, D_MODEL, eb), lambda i, e: (e, 0, 0)),
            pl.BlockSpec((1, D_MODEL), const2),
        ],
        out_specs=pl.BlockSpec((tt, D_MODEL), tok),
        out_shape=jax.ShapeDtypeStruct((n, D_MODEL), F32),
        scratch_shapes=[stat(F32), stat(F32), stat(jnp.uint32), stat(jnp.uint32),
                        stat(BF16), stat(BF16), pltpu.VMEM((D_MODEL, tt), F32)],
        compiler_params=pltpu.CompilerParams(
            dimension_semantics=("arbitrary", "arbitrary"),
            vmem_limit_bytes=VMEM_LIMIT_BYTES),
        name="peer",
    )(xn, x2, wq, keys, u, vt, final_w)


def _pad_lanes(v):
    return jnp.pad(v, (0, LANES - v.shape[0])).reshape(1, LANES)


def _tile(n, want):
    return want if n % want == 0 else n


def kernel(x, norm_mix_w, w_in, hg_lb_logits, hg_norm_w, ssd_conv_w, ssd_conv_b,
           ssd_dt_bias, ssd_A_log, ssd_D, ssd_norm_w, w_out, norm_ffn_w,
           peer_w_q, peer_keys, peer_u, peer_v, final_norm_w):
    batch, seq, d = x.shape
    n = batch * seq
    x2d = x.reshape(n, d)
    width = SSD_GROUPS * SSD_GROUP_WIDTH

    w_main = w_in[0][:, :PROJ_MAIN].astype(BF16)
    w_dt = jnp.pad(w_in[0][:, PROJ_MAIN:], ((0, 0), (0, LANES - SSD_HEADS))).astype(BF16)
    conv_w = ssd_conv_w[0]
    conv_b = ssd_conv_b[0].reshape(1, -1)
    w_out1 = w_out[0][:HG_HEADS * HG_HEAD_DIM].astype(BF16)
    w_out2 = w_out[0][HG_HEADS * HG_HEAD_DIM:].astype(BF16)
    keys = peer_keys[0].reshape(PEER_HEADS * 2, PEER_KEYS, -1).astype(BF16)
    u = peer_u[0].astype(BF16)
    vt = peer_v[0].astype(BF16).reshape(-1, 1024, D_MODEL).transpose(0, 2, 1)

    proj, dt_raw = _inproj(x2d, norm_mix_w[0].reshape(1, d), w_main, w_dt,
                           _tile(n, 1024), PROJ_MAIN // 4)
    o_hg = _hgrn2(proj, hg_lb_logits, hg_norm_w[0].reshape(1, -1), batch, seq, _tile(seq, 512))
    o_ssd = _ssd(proj, dt_raw, conv_w[:, :width], conv_w[:, width:], conv_b[:, :width],
                 conv_b[:, width:], _pad_lanes(ssd_dt_bias[0]), _pad_lanes(ssd_A_log[0]),
                 jnp.repeat(ssd_D[0], SSD_HEAD_DIM).reshape(1, width),
                 ssd_norm_w[0].reshape(1, width), batch, seq, _tile(seq, 256))
    x2, xn = _outproj(x2d, o_hg, o_ssd, w_out1, w_out2, norm_ffn_w[0].reshape(1, d),
                      _tile(n, 512))
    out = _peer(xn, x2, peer_w_q[0].astype(BF16), keys, u, vt, final_norm_w.reshape(1, d),
                _tile(n, 512), 1024, 256)
    return out.reshape(batch, seq, d)
```

```python
import functools
import math

import jax
import jax.numpy as jnp
from jax import lax
from jax.experimental import pallas as pl
from jax.experimental.pallas import tpu as pltpu

F32 = jnp.float32
BF16 = jnp.bfloat16
EPS = 1e-6
HIGHEST = lax.Precision.HIGHEST

D_MODEL = 1024
HG_HEADS = 8
HG_HEAD_DIM = 128
CHUNK = 64
SSD_HEADS = 16
SSD_HEAD_DIM = 64
SSD_GROUPS = 2
SSD_STATE = 128
SSD_CONV = 4
SSD_GROUP_WIDTH = 512
PEER_HEADS = 8
PEER_KEYS = 128
PEER_TOPK = 16
PROJ_MAIN = 6656
LANES = 128

VMEM_LIMIT_BYTES = 56 * 1024 * 1024


def _dot(a, b):
    return jnp.dot(a, b, preferred_element_type=F32)


def _dot_nt(a, b):
    return lax.dot_general(a, b, (((1,), (1,)), ((), ())), preferred_element_type=F32)


def _dot_tn(a, b):
    return lax.dot_general(a, b, (((0,), (0,)), ((), ())), preferred_element_type=F32)


def _split3(x):
    hi = x.astype(BF16)
    r1 = x - hi.astype(F32)
    mid = r1.astype(BF16)
    lo = (r1 - mid.astype(F32)).astype(BF16)
    return hi, mid, lo


def _dot_exact(a, b, exact_side):
    if exact_side == "a":
        a16 = a.astype(BF16)
        return sum(_dot(a16, piece) for piece in _split3(b))
    b16 = b.astype(BF16)
    return sum(_dot(piece, b16) for piece in _split3(a))


def _sigmoid(x):
    return 1.0 / (1.0 + jnp.exp(-x))


def _silu(x):
    return x * _sigmoid(x)


def _softplus(x):
    return jnp.maximum(x, 0.0) + jnp.log(1.0 + jnp.exp(-jnp.abs(x)))


def _gelu(x):
    return 0.5 * x * (1.0 + lax.erf(x * (1.0 / math.sqrt(2.0))))


def _tri(n, m=None):
    m = n if m is None else m
    r = lax.broadcasted_iota(jnp.int32, (n, m), 0)
    c = lax.broadcasted_iota(jnp.int32, (n, m), 1)
    return r >= (c % n)


def _inproj_body(x_ref, nw_ref, w_ref, wdt_ref, o_ref, dt_ref, hn_ref):
    @pl.when(pl.program_id(1) == 0)
    def _():
        x = x_ref[...]
        ms = jnp.mean(x * x, axis=-1, keepdims=True)
        hn = (x * lax.rsqrt(ms + EPS) * nw_ref[...]).astype(BF16)
        hn_ref[...] = hn
        dt_ref[...] = _dot(hn, wdt_ref[...])

    o_ref[...] = _dot(hn_ref[...], w_ref[...]).astype(o_ref.dtype)


def _inproj(x2d, norm_w, w_main, w_dt, tm, tn):
    n = x2d.shape[0]
    return pl.pallas_call(
        _inproj_body,
        grid=(n // tm, PROJ_MAIN // tn),
        in_specs=[
            pl.BlockSpec((tm, D_MODEL), lambda i, j: (i, 0)),
            pl.BlockSpec((1, D_MODEL), lambda i, j: (0, 0)),
            pl.BlockSpec((D_MODEL, tn), lambda i, j: (0, j)),
            pl.BlockSpec((D_MODEL, LANES), lambda i, j: (0, 0)),
        ],
        out_specs=[
            pl.BlockSpec((tm, tn), lambda i, j: (i, j)),
            pl.BlockSpec((tm, LANES), lambda i, j: (i, 0)),
        ],
        out_shape=[
            jax.ShapeDtypeStruct((n, PROJ_MAIN), BF16),
            jax.ShapeDtypeStruct((n, LANES), F32),
        ],
        scratch_shapes=[pltpu.VMEM((tm, D_MODEL), BF16)],
        compiler_params=pltpu.CompilerParams(
            dimension_semantics=("arbitrary", "arbitrary"),
            vmem_limit_bytes=VMEM_LIMIT_BYTES),
        name="inproj",
    )(x2d, norm_w, w_main, w_dt)


def _hgrn2_body(q_ref, f_ref, i_ref, g_ref, lbl_ref, nw_ref, o_ref, st_ref, *, n_chunks):
    @pl.when(pl.program_id(1) == 0)
    def _():
        st_ref[...] = jnp.zeros_like(st_ref)

    lbl = lbl_ref[...]
    e = jnp.exp(lbl - jnp.max(lbl, axis=0, keepdims=True))
    lb = e[0:1, :] / jnp.sum(e, axis=0, keepdims=True)
    tri = _tri(CHUNK)
    tri_f = tri.astype(F32)
    nw = nw_ref[...]
    hd = HG_HEAD_DIM

    def chunk(c, carry):
        sl = pl.ds(pl.multiple_of(c * CHUNK, CHUNK), CHUNK)
        f = lb + (1.0 - lb) * _sigmoid(f_ref[sl, :].astype(F32))
        k = 1.0 - f
        g_cum = _dot_exact(tri_f, jnp.log(f), "a")
        g_last = g_cum[CHUNK - 1:CHUNK, :]
        q_dec = (q_ref[sl, :].astype(F32) * jnp.exp(g_cum)).astype(BF16)
        k_dec = (k * jnp.exp(-g_cum)).astype(BF16)
        k_end = (k * jnp.exp(g_last - g_cum)).astype(BF16)
        decay = jnp.exp(g_last)
        v = i_ref[sl, :]
        gate = _silu(g_ref[sl, :].astype(F32))
        for h in range(HG_HEADS):
            hs = slice(h * hd, (h + 1) * hd)
            scores = jnp.where(tri, _dot_nt(q_dec[:, hs], k_dec[:, hs]), 0.0)
            st = st_ref[h]
            o = _dot(scores.astype(BF16), v[:, hs]) + _dot_nt(q_dec[:, hs], st.astype(BF16))
            st_ref[h] = st * decay[:, hs] + _dot_tn(v[:, hs], k_end[:, hs])
            ms = jnp.mean(o * o, axis=-1, keepdims=True)
            o_ref[sl, hs] = (o * lax.rsqrt(ms + EPS) * nw * gate[:, hs]).astype(o_ref.dtype)
        return carry

    lax.fori_loop(0, n_chunks, chunk, 0)


def _hgrn2(proj, lb_logits, norm_w, batch, seq, ts):
    n = proj.shape[0]
    nt = seq // ts
    hd = HG_HEAD_DIM
    width = HG_HEADS * hd

    def col(k):
        return pl.BlockSpec((ts, width), lambda b, t: (b * nt + t, k))

    return pl.pallas_call(
        functools.partial(_hgrn2_body, n_chunks=ts // CHUNK),
        grid=(batch, nt),
        in_specs=[
            col(0), col(1), col(2), col(3),
            pl.BlockSpec((2, width), lambda b, t: (0, 0)),
            pl.BlockSpec((1, hd), lambda b, t: (0, 0)),
        ],
        out_specs=pl.BlockSpec((ts, width), lambda b, t: (b * nt + t, 0)),
        out_shape=jax.ShapeDtypeStruct((n, width), BF16),
        scratch_shapes=[pltpu.VMEM((HG_HEADS, hd, hd), F32)],
        compiler_params=pltpu.CompilerParams(
            dimension_semantics=("arbitrary", "arbitrary"),
            vmem_limit_bytes=VMEM_LIMIT_BYTES),
        name="hgrn2",
    )(proj, proj, proj, proj, lb_logits, norm_w)


def _ssd_body(z_ref, xs_ref, bc_ref, dt_ref, cwx_ref, cwb_ref, cbx_ref, cbb_ref,
              dtb_ref, alog_ref, dx_ref, nw_ref, o_ref,
              xpad_ref, bpad_ref, xc_ref, bcc_ref, st_ref, *, ts):
    width = SSD_GROUPS * SSD_GROUP_WIDTH
    bcw = 2 * SSD_GROUPS * SSD_STATE

    @pl.when(pl.program_id(1) == 0)
    def _():
        st_ref[...] = jnp.zeros_like(st_ref)
        xpad_ref[0:8, :] = jnp.zeros((8, width), F32)
        bpad_ref[0:8, :] = jnp.zeros((8, bcw), F32)

    xpad_ref[8:, :] = xs_ref[...].astype(F32)
    bpad_ref[8:, :] = bc_ref[...].astype(F32)
    accx = cbx_ref[...]
    accb = cbb_ref[...]
    for j in range(SSD_CONV):
        lo = 8 - (SSD_CONV - 1) + j
        accx = accx + xpad_ref[lo:lo + ts, :] * cwx_ref[j:j + 1, :]
        accb = accb + bpad_ref[lo:lo + ts, :] * cwb_ref[j:j + 1, :]
    xc_ref[...] = _silu(accx)
    bcc_ref[...] = _silu(accb)
    xpad_ref[0:8, :] = xpad_ref[ts:ts + 8, :]
    bpad_ref[0:8, :] = bpad_ref[ts:ts + 8, :]

    a_row = -jnp.exp(alog_ref[...])
    dtb = dtb_ref[...]
    tri_f = _tri(CHUNK).astype(F32)
    tri2 = _tri(CHUNK, 2 * CHUNK)
    lane = lax.broadcasted_iota(jnp.int32, (1, LANES), 1)
    left = lane < SSD_HEAD_DIM
    hrow = lax.broadcasted_iota(jnp.int32, (LANES, width), 0)
    hcol = lax.broadcasted_iota(jnp.int32, (LANES, width), 1) // SSD_HEAD_DIM
    expand = (hrow == hcol).astype(F32)

    def chunk(c, carry):
        sl = pl.ds(pl.multiple_of(c * CHUNK, CHUNK), CHUNK)
        dt = _softplus(dt_ref[sl, :] + dtb)
        cum = _dot_exact(tri_f, dt * a_row, "a")
        cum_x = _dot_exact(cum, expand, "b")
        dt_x = _dot_exact(dt, expand, "b")
        cum2 = jnp.concatenate([cum, cum], axis=0)
        cum2_t = cum2.T
        xc = xc_ref[sl, :]
        xdt = xc * dt_x
        last_x = cum_x[CHUNK - 1:CHUNK, :]
        xd_end = (xdt * jnp.exp(last_x - cum_x)).astype(BF16)
        ecum_x = jnp.exp(cum_x)
        dec_x = jnp.exp(last_x)
        xdt_b = xdt.astype(BF16)
        ys = []
        for g in range(SSD_GROUPS):
            gs = slice(g * SSD_GROUP_WIDTH, (g + 1) * SSD_GROUP_WIDTH)
            bm = bcc_ref[sl, g * SSD_STATE:(g + 1) * SSD_STATE].astype(BF16)
            cm = bcc_ref[sl, (SSD_GROUPS + g) * SSD_STATE:
                         (SSD_GROUPS + g + 1) * SSD_STATE].astype(BF16)
            cb2 = _dot_nt(cm, jnp.concatenate([bm, bm], axis=0))
            st = st_ref[g]
            y_off = _dot(cm, st.astype(BF16)) * ecum_x[:, gs]
            st_ref[g] = st * dec_x[:, gs] + _dot_tn(bm, xd_end[:, gs])
            pieces = []
            for p in range(SSD_GROUP_WIDTH // LANES):
                h0 = g * (SSD_HEADS // SSD_GROUPS) + 2 * p
                cs = slice(g * SSD_GROUP_WIDTH + p * LANES, g * SSD_GROUP_WIDTH + (p + 1) * LANES)
                col = cum_x[:, cs]
                row = jnp.where(left, cum2_t[h0:h0 + 1, :], cum2_t[h0 + 1:h0 + 2, :])
                decay = jnp.exp(jnp.where(tri2, col - row, -jnp.inf))
                m2 = (cb2 * decay).astype(BF16)
                xp = xdt_b[:, cs]
                zero = jnp.zeros_like(xp)
                x2 = jnp.concatenate([jnp.where(left, xp, zero), jnp.where(left, zero, xp)],
                                     axis=0)
                pieces.append(_dot(m2, x2))
            ys.append(jnp.concatenate(pieces, axis=1) + y_off)
        y = jnp.concatenate(ys, axis=1) + dx_ref[...] * xc
        y = y * _silu(z_ref[sl, :].astype(F32))
        nw = nw_ref[...]
        for g in range(SSD_GROUPS):
            gs = slice(g * SSD_GROUP_WIDTH, (g + 1) * SSD_GROUP_WIDTH)
            yg = y[:, gs]
            ms = jnp.mean(yg * yg, axis=-1, keepdims=True)
            o_ref[sl, gs] = (yg * lax.rsqrt(ms + EPS) * nw[:, gs]).astype(o_ref.dtype)
        return carry

    lax.fori_loop(0, ts // CHUNK, chunk, 0, unroll=2)


def _ssd(proj, dt_raw, conv_wx, conv_wb, conv_bx, conv_bb, dt_bias, a_log, d_x, norm_w,
         batch, seq, ts):
    n = proj.shape[0]
    nt = seq // ts
    width = SSD_GROUPS * SSD_GROUP_WIDTH
    bcw = 2 * SSD_GROUPS * SSD_STATE

    def full(shape):
        return pl.BlockSpec(shape, lambda b, t: (0, 0))

    return pl.pallas_call(
        functools.partial(_ssd_body, ts=ts),
        grid=(batch, nt),
        in_specs=[
            pl.BlockSpec((ts, width), lambda b, t: (b * nt + t, 4)),
            pl.BlockSpec((ts, width), lambda b, t: (b * nt + t, 5)),
            pl.BlockSpec((ts, bcw), lambda b, t: (b * nt + t, 12)),
            pl.BlockSpec((ts, LANES), lambda b, t: (b * nt + t, 0)),
            full((SSD_CONV, width)), full((SSD_CONV, bcw)),
            full((1, width)), full((1, bcw)),
            full((1, LANES)), full((1, LANES)),
            full((1, width)), full((1, width)),
        ],
        out_specs=pl.BlockSpec((ts, width), lambda b, t: (b * nt + t, 0)),
        out_shape=jax.ShapeDtypeStruct((n, width), BF16),
        scratch_shapes=[
            pltpu.VMEM((ts + 8, width), F32),
            pltpu.VMEM((ts + 8, bcw), F32),
            pltpu.VMEM((ts, width), F32),
            pltpu.VMEM((ts, bcw), F32),
            pltpu.VMEM((SSD_GROUPS, SSD_STATE, SSD_GROUP_WIDTH), F32),
        ],
        compiler_params=pltpu.CompilerParams(
            dimension_semantics=("arbitrary", "arbitrary"),
            vmem_limit_bytes=VMEM_LIMIT_BYTES),
        name="ssd",
    )(proj, proj, proj, dt_raw, conv_wx, conv_wb, conv_bx, conv_bb, dt_bias, a_log, d_x, norm_w)


def _outproj_body(x_ref, hg_ref, ssd_ref, w1_ref, w2_ref, nw_ref, x2_ref, xn_ref):
    x2 = x_ref[...] + _dot(hg_ref[...], w1_ref[...]) + _dot(ssd_ref[...], w2_ref[...])
    x2_ref[...] = x2
    ms = jnp.mean(x2 * x2, axis=-1, keepdims=True)
    xn_ref[...] = (x2 * lax.rsqrt(ms + EPS) * nw_ref[...]).astype(BF16)


def _outproj(x2d, o_hg, o_ssd, w1, w2, norm_w, tm):
    n = x2d.shape[0]
    row = pl.BlockSpec((tm, D_MODEL), lambda i: (i, 0))
    wfull = pl.BlockSpec((D_MODEL, D_MODEL), lambda i: (0, 0))
    return pl.pallas_call(
        _outproj_body,
        grid=(n // tm,),
        in_specs=[row, row, row, wfull, wfull, pl.BlockSpec((1, D_MODEL), lambda i: (0, 0))],
        out_specs=[row, row],
        out_shape=[jax.ShapeDtypeStruct((n, D_MODEL), F32),
                   jax.ShapeDtypeStruct((n, D_MODEL), BF16)],
        compiler_params=pltpu.CompilerParams(
            dimension_semantics=("arbitrary",), vmem_limit_bytes=VMEM_LIMIT_BYTES),
        name="outproj",
    )(x2d, o_hg, o_ssd, w1, w2, norm_w)


def _top_values(s, k):
    vals = []
    for _ in range(k):
        mx = jnp.max(s, axis=0, keepdims=True)
        vals.append(mx)
        s = jnp.where(s >= mx, -jnp.inf, s)
    return vals


def _candidate_sums(a, b):
    a16 = jnp.concatenate(a, axis=0)
    b16 = jnp.concatenate(b, axis=0)
    b8 = b16[0:8]
    rank = lax.broadcasted_iota(jnp.int32, b8.shape, 0) + 1
    rows = [a[0] + b16, a[1] + b8]
    for k in range(3, 9):
        rows.append(jnp.where(rank * k <= PEER_TOPK, a[k - 1] + b8, -jnp.inf))
    rows.append(a16[8:16] + b[0])
    return jnp.concatenate(rows, axis=0)


def _odd_even_merge_sort(lo, hi):
    def merge(lo, hi, r):
        step = 2 * r
        if step < hi - lo:
            yield from merge(lo, hi, step)
            yield from merge(lo + r, hi, step)
            for i in range(lo + r, hi - r, step):
                yield (i, i + r)
        else:
            yield (lo, lo + r)

    if hi > lo:
        mid = lo + (hi - lo) // 2
        yield from _odd_even_merge_sort(lo, mid)
        yield from _odd_even_merge_sort(mid + 1, hi)
        yield from merge(lo, hi, 1)


def _sorted_top16(s):
    n = PEER_TOPK
    v = [s[r * 8:(r + 1) * 8, :] for r in range(n)]
    for a, b in _odd_even_merge_sort(0, n - 1):
        v[a], v[b] = jnp.maximum(v[a], v[b]), jnp.minimum(v[a], v[b])
    for shift in (4, 2, 1):
        other = [pltpu.roll(x, shift, axis=0) for x in v]
        v = [jnp.maximum(v[r], other[n - 1 - r]) for r in range(n)]
        d = n // 2
        while d >= 1:
            for i in range(n):
                if i & d == 0:
                    v[i], v[i + d] = jnp.maximum(v[i], v[i + d]), jnp.minimum(v[i], v[i + d])
            d //= 2
    return [x[0:1, :] for x in v]


def _ranks(s, top):
    rank = jnp.full(s.shape, 2.0 * len(top), F32)
    for l in range(len(top), 0, -1):
        rank = jnp.where(s >= top[l - 1], float(l), rank)
    return rank


def _pair_of_bf16(x):
    bits = lax.bitcast_convert_type(x.astype(BF16).astype(F32), jnp.uint32) >> 16
    return bits | (bits << 16)


def _packed_row(tile, s):
    return pltpu.bitcast(jnp.broadcast_to(tile[s:s + 1, :], (8, LANES)), BF16)


def _peer_body(xn_ref, x2_ref, wq_ref, keys_ref, u_ref, vt_ref, fw_ref, o_ref,
               s1_ref, s2_ref, n_ref, c1_ref, r2_ref, e2_ref, acc_ref, *, tt, eb, eq):
    e = pl.program_id(1)
    ib = eb // PEER_KEYS
    n_tc = tt // LANES
    pk = 16

    @pl.when(e == 0)
    def _():
        q = _dot(xn_ref[...], wq_ref[...]).astype(BF16)
        for h in range(PEER_HEADS):
            for p, ref in ((0, s1_ref), (1, s2_ref)):
                c0 = (2 * h + p) * PEER_KEYS
                s = _dot_nt(keys_ref[2 * h + p], q[:, c0:c0 + PEER_KEYS])
                for tc in range(n_tc):
                    ref[h * n_tc + tc] = s[:, tc * LANES:(tc + 1) * LANES]

        def stats(k, carry):
            s1 = s1_ref[k]
            s2 = s2_ref[k]
            a = _sorted_top16(s1)
            b = _sorted_top16(s2)
            rank2 = _ranks(s2, b)
            cand = _candidate_sums(a, b)
            tau = _top_values(cand, PEER_TOPK)[-1]
            m = a[0] + b[0]
            z = jnp.sum(jnp.where(cand >= tau, jnp.exp(cand - m), 0.0), axis=0, keepdims=True)
            count = jnp.zeros_like(s1)
            for l in range(PEER_TOPK):
                count = jnp.where(s1 + b[l] >= tau, l + 1.0, count)
            n_ref[k] = _pair_of_bf16(count)
            c1_ref[k] = _pair_of_bf16(jnp.exp(s1 - a[0]))
            r2_ref[k] = rank2.astype(BF16)
            e2_ref[k] = (jnp.exp(s2 - b[0]) / z).astype(BF16)
            return carry

        lax.fori_loop(0, PEER_HEADS * n_tc, stats, 0)
        acc_ref[...] = jnp.zeros_like(acc_ref)

    xn = xn_ref[...]

    def pre_activations(r0):
        return _dot_nt(u_ref[r0:r0 + eq, :], xn)

    hq_next = pre_activations(0)
    at_prev = None
    for r0 in range(0, eb, eq):
        hq = hq_next
        if r0 + eq < eb:
            hq_next = pre_activations(r0 + eq)
        if at_prev is not None:
            acc_ref[...] += _dot(vt_ref[0, :, r0 - eq:r0], at_prev)
        n_ii = eq // PEER_KEYS
        n_jv = PEER_KEYS // pk
        zero = jnp.zeros((pk, LANES), BF16)
        blocks = [[None] * n_tc for _ in range(n_ii)]
        for tc in range(n_tc):
            w = [[None] * n_jv for _ in range(n_ii)]
            for h in range(PEER_HEADS):
                k = h * n_tc + tc
                rank2 = [r2_ref[k, jv * pk:(jv + 1) * pk, :] for jv in range(n_jv)]
                e2 = [e2_ref[k, jv * pk:(jv + 1) * pk, :] for jv in range(n_jv)]
                for ii in range(n_ii):
                    s = r0 // PEER_KEYS + ii
                    rows8 = pl.ds(pl.multiple_of(e * ib + (s // 8) * 8, 8), 8)
                    count = _packed_row(n_ref[k, rows8, :], s % 8)
                    c1 = _packed_row(c1_ref[k, rows8, :], s % 8)
                    for jv in range(n_jv):
                        g = jnp.where(rank2[jv] <= count, e2[jv], zero) * c1
                        w[ii][jv] = g if h == 0 else w[ii][jv] + g
            for ii in range(n_ii):
                hh = hq[ii * PEER_KEYS:(ii + 1) * PEER_KEYS, tc * LANES:(tc + 1) * LANES]
                blocks[ii][tc] = _gelu(hh).astype(BF16) * jnp.concatenate(w[ii], axis=0)
        at_prev = jnp.concatenate([jnp.concatenate(row, axis=1) for row in blocks],
                                  axis=0)
    acc_ref[...] += _dot(vt_ref[0, :, eb - eq:eb], at_prev)

    @pl.when(e == pl.num_programs(1) - 1)
    def _():
        y = x2_ref[...] + acc_ref[...].T
        ms = jnp.mean(y * y, axis=-1, keepdims=True)
        o_ref[...] = y * lax.rsqrt(ms + EPS) * fw_ref[...]


def _peer(xn, x2, wq, keys, u, vt, final_w, tt, eb, eq):
    n = xn.shape[0]
    n_exp = u.shape[0]
    tok = lambda i, e: (i, 0)
    const2 = lambda i, e: (0, 0)
    def stat(dtype):
        return pltpu.VMEM((PEER_HEADS * (tt // LANES), PEER_KEYS, LANES), dtype)

    return pl.pallas_call(
        functools.partial(_peer_body, tt=tt, eb=eb, eq=eq),
        grid=(n // tt, n_exp // eb),
        in_specs=[
            pl.BlockSpec((tt, D_MODEL), tok),
            pl.BlockSpec((tt, D_MODEL), tok),
            pl.BlockSpec(wq.shape, const2),
            pl.BlockSpec(keys.shape, lambda i, e: (0, 0, 0)),
            pl.BlockSpec((eb, D_MODEL), lambda i, e: (e, 0)),
            pl.BlockSpec((1, D_MODEL, eb), lambda i, e: (e, 0, 0)),
            pl.BlockSpec((1, D_MODEL), const2),
        ],
        out_specs=pl.BlockSpec((tt, D_MODEL), tok),
        out_shape=jax.ShapeDtypeStruct((n, D_MODEL), F32),
        scratch_shapes=[stat(F32), stat(F32), stat(jnp.uint32), stat(jnp.uint32),
                        stat(BF16), stat(BF16), pltpu.VMEM((D_MODEL, tt), F32)],
        compiler_params=pltpu.CompilerParams(
            dimension_semantics=("arbitrary", "arbitrary"),
            vmem_limit_bytes=VMEM_LIMIT_BYTES),
        name="peer",
    )(xn, x2, wq, keys, u, vt, final_w)


def _pad_lanes(v):
    return jnp.pad(v, (0, LANES - v.shape[0])).reshape(1, LANES)


def _tile(n, want):
    return want if n % want == 0 else n


def kernel(x, norm_mix_w, w_in, hg_lb_logits, hg_norm_w, ssd_conv_w, ssd_conv_b,
           ssd_dt_bias, ssd_A_log, ssd_D, ssd_norm_w, w_out, norm_ffn_w,
           peer_w_q, peer_keys, peer_u, peer_v, final_norm_w):
    batch, seq, d = x.shape
    n = batch * seq
    x2d = x.reshape(n, d)
    width = SSD_GROUPS * SSD_GROUP_WIDTH

    w_main = w_in[0][:, :PROJ_MAIN].astype(BF16)
    w_dt = jnp.pad(w_in[0][:, PROJ_MAIN:], ((0, 0), (0, LANES - SSD_HEADS))).astype(BF16)
    conv_w = ssd_conv_w[0]
    conv_b = ssd_conv_b[0].reshape(1, -1)
    w_out1 = w_out[0][:HG_HEADS * HG_HEAD_DIM].astype(BF16)
    w_out2 = w_out[0][HG_HEADS * HG_HEAD_DIM:].astype(BF16)
    keys = peer_keys[0].reshape(PEER_HEADS * 2, PEER_KEYS, -1).astype(BF16)
    u = peer_u[0].astype(BF16)
    vt = peer_v[0].astype(BF16).reshape(-1, 1024, D_MODEL).transpose(0, 2, 1)

    proj, dt_raw = _inproj(x2d, norm_mix_w[0].reshape(1, d), w_main, w_dt,
                           _tile(n, 1024), PROJ_MAIN // 4)
    o_hg = _hgrn2(proj, hg_lb_logits, hg_norm_w[0].reshape(1, -1), batch, seq, _tile(seq, 512))
    o_ssd = _ssd(proj, dt_raw, conv_w[:, :width], conv_w[:, width:], conv_b[:, :width],
                 conv_b[:, width:], _pad_lanes(ssd_dt_bias[0]), _pad_lanes(ssd_A_log[0]),
                 jnp.repeat(ssd_D[0], SSD_HEAD_DIM).reshape(1, width),
                 ssd_norm_w[0].reshape(1, width), batch, seq, _tile(seq, 256))
    x2, xn = _outproj(x2d, o_hg, o_ssd, w_out1, w_out2, norm_ffn_w[0].reshape(1, d),
                      _tile(n, 512))
    out = _peer(xn, x2, peer_w_q[0].astype(BF16), keys, u, vt, final_norm_w.reshape(1, d),
                _tile(n, 512), 1024, 512)
    return out.reshape(batch, seq, d)
```

```python
import functools
import math

import jax
import jax.numpy as jnp
from jax import lax
from jax.experimental import pallas as pl
from jax.experimental.pallas import tpu as pltpu

F32 = jnp.float32
BF16 = jnp.bfloat16
EPS = 1e-6
HIGHEST = lax.Precision.HIGHEST

D_MODEL = 1024
HG_HEADS = 8
HG_HEAD_DIM = 128
CHUNK = 64
SSD_HEADS = 16
SSD_HEAD_DIM = 64
SSD_GROUPS = 2
SSD_STATE = 128
SSD_CONV = 4
SSD_GROUP_WIDTH = 512
PEER_HEADS = 8
PEER_KEYS = 128
PEER_TOPK = 16
PROJ_MAIN = 6656
LANES = 128

VMEM_LIMIT_BYTES = 56 * 1024 * 1024


def _dot(a, b):
    return jnp.dot(a, b, preferred_element_type=F32)


def _dot_nt(a, b):
    return lax.dot_general(a, b, (((1,), (1,)), ((), ())), preferred_element_type=F32)


def _dot_tn(a, b):
    return lax.dot_general(a, b, (((0,), (0,)), ((), ())), preferred_element_type=F32)


def _split3(x):
    hi = x.astype(BF16)
    r1 = x - hi.astype(F32)
    mid = r1.astype(BF16)
    lo = (r1 - mid.astype(F32)).astype(BF16)
    return hi, mid, lo


def _dot_exact(a, b, exact_side):
    if exact_side == "a":
        a16 = a.astype(BF16)
        return sum(_dot(a16, piece) for piece in _split3(b))
    b16 = b.astype(BF16)
    return sum(_dot(piece, b16) for piece in _split3(a))


def _sigmoid(x):
    return 1.0 / (1.0 + jnp.exp(-x))


def _silu(x):
    return x * _sigmoid(x)


def _softplus(x):
    return jnp.maximum(x, 0.0) + jnp.log(1.0 + jnp.exp(-jnp.abs(x)))


def _gelu(x):
    return 0.5 * x * (1.0 + lax.erf(x * (1.0 / math.sqrt(2.0))))


def _tri(n, m=None):
    m = n if m is None else m
    r = lax.broadcasted_iota(jnp.int32, (n, m), 0)
    c = lax.broadcasted_iota(jnp.int32, (n, m), 1)
    return r >= (c % n)


def _inproj_body(x_ref, nw_ref, w_ref, wdt_ref, o_ref, dt_ref, hn_ref):
    @pl.when(pl.program_id(1) == 0)
    def _():
        x = x_ref[...]
        ms = jnp.mean(x * x, axis=-1, keepdims=True)
        hn = (x * lax.rsqrt(ms + EPS) * nw_ref[...]).astype(BF16)
        hn_ref[...] = hn
        dt_ref[...] = _dot(hn, wdt_ref[...])

    o_ref[...] = _dot(hn_ref[...], w_ref[...]).astype(o_ref.dtype)


def _inproj(x2d, norm_w, w_main, w_dt, tm, tn):
    n = x2d.shape[0]
    return pl.pallas_call(
        _inproj_body,
        grid=(n // tm, PROJ_MAIN // tn),
        in_specs=[
            pl.BlockSpec((tm, D_MODEL), lambda i, j: (i, 0)),
            pl.BlockSpec((1, D_MODEL), lambda i, j: (0, 0)),
            pl.BlockSpec((D_MODEL, tn), lambda i, j: (0, j)),
            pl.BlockSpec((D_MODEL, LANES), lambda i, j: (0, 0)),
        ],
        out_specs=[
            pl.BlockSpec((tm, tn), lambda i, j: (i, j)),
            pl.BlockSpec((tm, LANES), lambda i, j: (i, 0)),
        ],
        out_shape=[
            jax.ShapeDtypeStruct((n, PROJ_MAIN), BF16),
            jax.ShapeDtypeStruct((n, LANES), F32),
        ],
        scratch_shapes=[pltpu.VMEM((tm, D_MODEL), BF16)],
        compiler_params=pltpu.CompilerParams(
            dimension_semantics=("arbitrary", "arbitrary"),
            vmem_limit_bytes=VMEM_LIMIT_BYTES),
        name="inproj",
    )(x2d, norm_w, w_main, w_dt)


def _hgrn2_body(q_ref, f_ref, i_ref, g_ref, lbl_ref, nw_ref, o_ref, st_ref, *, n_chunks):
    @pl.when(pl.program_id(1) == 0)
    def _():
        st_ref[...] = jnp.zeros_like(st_ref)

    lbl = lbl_ref[...]
    e = jnp.exp(lbl - jnp.max(lbl, axis=0, keepdims=True))
    lb = e[0:1, :] / jnp.sum(e, axis=0, keepdims=True)
    tri = _tri(CHUNK)
    tri_f = tri.astype(F32)
    nw = nw_ref[...]
    hd = HG_HEAD_DIM

    def chunk(c, carry):
        sl = pl.ds(pl.multiple_of(c * CHUNK, CHUNK), CHUNK)
        f = lb + (1.0 - lb) * _sigmoid(f_ref[sl, :].astype(F32))
        k = 1.0 - f
        g_cum = _dot_exact(tri_f, jnp.log(f), "a")
        g_last = g_cum[CHUNK - 1:CHUNK, :]
        q_dec = (q_ref[sl, :].astype(F32) * jnp.exp(g_cum)).astype(BF16)
        k_dec = (k * jnp.exp(-g_cum)).astype(BF16)
        k_end = (k * jnp.exp(g_last - g_cum)).astype(BF16)
        decay = jnp.exp(g_last)
        v = i_ref[sl, :]
        gate = _silu(g_ref[sl, :].astype(F32))
        for h in range(HG_HEADS):
            hs = slice(h * hd, (h + 1) * hd)
            scores = jnp.where(tri, _dot_nt(q_dec[:, hs], k_dec[:, hs]), 0.0)
            st = st_ref[h]
            o = _dot(scores.astype(BF16), v[:, hs]) + _dot_nt(q_dec[:, hs], st.astype(BF16))
            st_ref[h] = st * decay[:, hs] + _dot_tn(v[:, hs], k_end[:, hs])
            ms = jnp.mean(o * o, axis=-1, keepdims=True)
            o_ref[sl, hs] = (o * lax.rsqrt(ms + EPS) * nw * gate[:, hs]).astype(o_ref.dtype)
        return carry

    lax.fori_loop(0, n_chunks, chunk, 0)


def _hgrn2(proj, lb_logits, norm_w, batch, seq, ts):
    n = proj.shape[0]
    nt = seq // ts
    hd = HG_HEAD_DIM
    width = HG_HEADS * hd

    def col(k):
        return pl.BlockSpec((ts, width), lambda b, t: (b * nt + t, k))

    return pl.pallas_call(
        functools.partial(_hgrn2_body, n_chunks=ts // CHUNK),
        grid=(batch, nt),
        in_specs=[
            col(0), col(1), col(2), col(3),
            pl.BlockSpec((2, width), lambda b, t: (0, 0)),
            pl.BlockSpec((1, hd), lambda b, t: (0, 0)),
        ],
        out_specs=pl.BlockSpec((ts, width), lambda b, t: (b * nt + t, 0)),
        out_shape=jax.ShapeDtypeStruct((n, width), BF16),
        scratch_shapes=[pltpu.VMEM((HG_HEADS, hd, hd), F32)],
        compiler_params=pltpu.CompilerParams(
            dimension_semantics=("arbitrary", "arbitrary"),
            vmem_limit_bytes=VMEM_LIMIT_BYTES),
        name="hgrn2",
    )(proj, proj, proj, proj, lb_logits, norm_w)


def _ssd_body(z_ref, xs_ref, bc_ref, dt_ref, cwx_ref, cwb_ref, cbx_ref, cbb_ref,
              dtb_ref, alog_ref, dx_ref, nw_ref, o_ref,
              xpad_ref, bpad_ref, xc_ref, bcc_ref, st_ref, *, ts):
    width = SSD_GROUPS * SSD_GROUP_WIDTH
    bcw = 2 * SSD_GROUPS * SSD_STATE

    @pl.when(pl.program_id(1) == 0)
    def _():
        st_ref[...] = jnp.zeros_like(st_ref)
        xpad_ref[0:8, :] = jnp.zeros((8, width), F32)
        bpad_ref[0:8, :] = jnp.zeros((8, bcw), F32)

    xpad_ref[8:, :] = xs_ref[...].astype(F32)
    bpad_ref[8:, :] = bc_ref[...].astype(F32)
    accx = cbx_ref[...]
    accb = cbb_ref[...]
    for j in range(SSD_CONV):
        lo = 8 - (SSD_CONV - 1) + j
        accx = accx + xpad_ref[lo:lo + ts, :] * cwx_ref[j:j + 1, :]
        accb = accb + bpad_ref[lo:lo + ts, :] * cwb_ref[j:j + 1, :]
    xc_ref[...] = _silu(accx)
    bcc_ref[...] = _silu(accb)
    xpad_ref[0:8, :] = xpad_ref[ts:ts + 8, :]
    bpad_ref[0:8, :] = bpad_ref[ts:ts + 8, :]

    a_row = -jnp.exp(alog_ref[...])
    dtb = dtb_ref[...]
    tri_f = _tri(CHUNK).astype(F32)
    tri2 = _tri(CHUNK, 2 * CHUNK)
    lane = lax.broadcasted_iota(jnp.int32, (1, LANES), 1)
    left = lane < SSD_HEAD_DIM
    hrow = lax.broadcasted_iota(jnp.int32, (LANES, width), 0)
    hcol = lax.broadcasted_iota(jnp.int32, (LANES, width), 1) // SSD_HEAD_DIM
    expand = (hrow == hcol).astype(F32)

    def chunk(c, carry):
        sl = pl.ds(pl.multiple_of(c * CHUNK, CHUNK), CHUNK)
        dt = _softplus(dt_ref[sl, :] + dtb)
        cum = _dot_exact(tri_f, dt * a_row, "a")
        cum_x = _dot_exact(cum, expand, "b")
        dt_x = _dot_exact(dt, expand, "b")
        cum2 = jnp.concatenate([cum, cum], axis=0)
        cum2_t = cum2.T
        xc = xc_ref[sl, :]
        xdt = xc * dt_x
        last_x = cum_x[CHUNK - 1:CHUNK, :]
        xd_end = (xdt * jnp.exp(last_x - cum_x)).astype(BF16)
        ecum_x = jnp.exp(cum_x)
        dec_x = jnp.exp(last_x)
        xdt_b = xdt.astype(BF16)
        ys = []
        for g in range(SSD_GROUPS):
            gs = slice(g * SSD_GROUP_WIDTH, (g + 1) * SSD_GROUP_WIDTH)
            bm = bcc_ref[sl, g * SSD_STATE:(g + 1) * SSD_STATE].astype(BF16)
            cm = bcc_ref[sl, (SSD_GROUPS + g) * SSD_STATE:
                         (SSD_GROUPS + g + 1) * SSD_STATE].astype(BF16)
            cb2 = _dot_nt(cm, jnp.concatenate([bm, bm], axis=0))
            st = st_ref[g]
            y_off = _dot(cm, st.astype(BF16)) * ecum_x[:, gs]
            st_ref[g] = st * dec_x[:, gs] + _dot_tn(bm, xd_end[:, gs])
            pieces = []
            for p in range(SSD_GROUP_WIDTH // LANES):
                h0 = g * (SSD_HEADS // SSD_GROUPS) + 2 * p
                cs = slice(g * SSD_GROUP_WIDTH + p * LANES, g * SSD_GROUP_WIDTH + (p + 1) * LANES)
                col = cum_x[:, cs]
                row = jnp.where(left, cum2_t[h0:h0 + 1, :], cum2_t[h0 + 1:h0 + 2, :])
                decay = jnp.exp(jnp.where(tri2, col - row, -jnp.inf))
                m2 = (cb2 * decay).astype(BF16)
                xp = xdt_b[:, cs]
                zero = jnp.zeros_like(xp)
                x2 = jnp.concatenate([jnp.where(left, xp, zero), jnp.where(left, zero, xp)],
                                     axis=0)
                pieces.append(_dot(m2, x2))
            ys.append(jnp.concatenate(pieces, axis=1) + y_off)
        y = jnp.concatenate(ys, axis=1) + dx_ref[...] * xc
        y = y * _silu(z_ref[sl, :].astype(F32))
        nw = nw_ref[...]
        for g in range(SSD_GROUPS):
            gs = slice(g * SSD_GROUP_WIDTH, (g + 1) * SSD_GROUP_WIDTH)
            yg = y[:, gs]
            ms = jnp.mean(yg * yg, axis=-1, keepdims=True)
            o_ref[sl, gs] = (yg * lax.rsqrt(ms + EPS) * nw[:, gs]).astype(o_ref.dtype)
        return carry

    lax.fori_loop(0, ts // CHUNK, chunk, 0, unroll=2)


def _ssd(proj, dt_raw, conv_wx, conv_wb, conv_bx, conv_bb, dt_bias, a_log, d_x, norm_w,
         batch, seq, ts):
    n = proj.shape[0]
    nt = seq // ts
    width = SSD_GROUPS * SSD_GROUP_WIDTH
    bcw = 2 * SSD_GROUPS * SSD_STATE

    def full(shape):
        return pl.BlockSpec(shape, lambda b, t: (0, 0))

    return pl.pallas_call(
        functools.partial(_ssd_body, ts=ts),
        grid=(batch, nt),
        in_specs=[
            pl.BlockSpec((ts, width), lambda b, t: (b * nt + t, 4)),
            pl.BlockSpec((ts, width), lambda b, t: (b * nt + t, 5)),
            pl.BlockSpec((ts, bcw), lambda b, t: (b * nt + t, 12)),
            pl.BlockSpec((ts, LANES), lambda b, t: (b * nt + t, 0)),
            full((SSD_CONV, width)), full((SSD_CONV, bcw)),
            full((1, width)), full((1, bcw)),
            full((1, LANES)), full((1, LANES)),
            full((1, width)), full((1, width)),
        ],
        out_specs=pl.BlockSpec((ts, width), lambda b, t: (b * nt + t, 0)),
        out_shape=jax.ShapeDtypeStruct((n, width), BF16),
        scratch_shapes=[
            pltpu.VMEM((ts + 8, width), F32),
            pltpu.VMEM((ts + 8, bcw), F32),
            pltpu.VMEM((ts, width), F32),
            pltpu.VMEM((ts, bcw), F32),
            pltpu.VMEM((SSD_GROUPS, SSD_STATE, SSD_GROUP_WIDTH), F32),
        ],
        compiler_params=pltpu.CompilerParams(
            dimension_semantics=("arbitrary", "arbitrary"),
            vmem_limit_bytes=VMEM_LIMIT_BYTES),
        name="ssd",
    )(proj, proj, proj, dt_raw, conv_wx, conv_wb, conv_bx, conv_bb, dt_bias, a_log, d_x, norm_w)


def _outproj_body(x_ref, hg_ref, ssd_ref, w1_ref, w2_ref, nw_ref, x2_ref, xn_ref):
    x2 = x_ref[...] + _dot(hg_ref[...], w1_ref[...]) + _dot(ssd_ref[...], w2_ref[...])
    x2_ref[...] = x2
    ms = jnp.mean(x2 * x2, axis=-1, keepdims=True)
    xn_ref[...] = (x2 * lax.rsqrt(ms + EPS) * nw_ref[...]).astype(BF16)


def _outproj(x2d, o_hg, o_ssd, w1, w2, norm_w, tm):
    n = x2d.shape[0]
    row = pl.BlockSpec((tm, D_MODEL), lambda i: (i, 0))
    wfull = pl.BlockSpec((D_MODEL, D_MODEL), lambda i: (0, 0))
    return pl.pallas_call(
        _outproj_body,
        grid=(n // tm,),
        in_specs=[row, row, row, wfull, wfull, pl.BlockSpec((1, D_MODEL), lambda i: (0, 0))],
        out_specs=[row, row],
        out_shape=[jax.ShapeDtypeStruct((n, D_MODEL), F32),
                   jax.ShapeDtypeStruct((n, D_MODEL), BF16)],
        compiler_params=pltpu.CompilerParams(
            dimension_semantics=("arbitrary",), vmem_limit_bytes=VMEM_LIMIT_BYTES),
        name="outproj",
    )(x2d, o_hg, o_ssd, w1, w2, norm_w)


def _top_values(s, k):
    vals = []
    for _ in range(k):
        mx = jnp.max(s, axis=0, keepdims=True)
        vals.append(mx)
        s = jnp.where(s >= mx, -jnp.inf, s)
    return vals


def _candidate_sums(a, b):
    a16 = jnp.concatenate(a, axis=0)
    b16 = jnp.concatenate(b, axis=0)
    b8 = b16[0:8]
    rank = lax.broadcasted_iota(jnp.int32, b8.shape, 0) + 1
    rows = [a[0] + b16, a[1] + b8]
    for k in range(3, 9):
        rows.append(jnp.where(rank * k <= PEER_TOPK, a[k - 1] + b8, -jnp.inf))
    rows.append(a16[8:16] + b[0])
    return jnp.concatenate(rows, axis=0)


def _odd_even_merge_sort(lo, hi):
    def merge(lo, hi, r):
        step = 2 * r
        if step < hi - lo:
            yield from merge(lo, hi, step)
            yield from merge(lo + r, hi, step)
            for i in range(lo + r, hi - r, step):
                yield (i, i + r)
        else:
            yield (lo, lo + r)

    if hi > lo:
        mid = lo + (hi - lo) // 2
        yield from _odd_even_merge_sort(lo, mid)
        yield from _odd_even_merge_sort(mid + 1, hi)
        yield from merge(lo, hi, 1)


def _sorted_top16(s):
    n = PEER_TOPK
    v = [s[r * 8:(r + 1) * 8, :] for r in range(n)]
    for a, b in _odd_even_merge_sort(0, n - 1):
        v[a], v[b] = jnp.maximum(v[a], v[b]), jnp.minimum(v[a], v[b])
    for shift in (4, 2, 1):
        other = [pltpu.roll(x, shift, axis=0) for x in v]
        v = [jnp.maximum(v[r], other[n - 1 - r]) for r in range(n)]
        d = n // 2
        while d >= 1:
            for i in range(n):
                if i & d == 0:
                    v[i], v[i + d] = jnp.maximum(v[i], v[i + d]), jnp.minimum(v[i], v[i + d])
            d //= 2
    return [x[0:1, :] for x in v]


def _ranks(s, top):
    rank = jnp.full(s.shape, 2.0 * len(top), F32)
    for l in range(len(top), 0, -1):
        rank = jnp.where(s >= top[l - 1], float(l), rank)
    return rank


def _pair_of_bf16(x):
    bits = lax.bitcast_convert_type(x.astype(BF16).astype(F32), jnp.uint32) >> 16
    return bits | (bits << 16)


def _packed_row(tile, s):
    return pltpu.bitcast(jnp.broadcast_to(tile[s:s + 1, :], (8, LANES)), BF16)


def _peer_body(xn_ref, x2_ref, wq_ref, keys_ref, u_ref, vt_ref, fw_ref, o_ref,
               s1_ref, s2_ref, n_ref, c1_ref, r2_ref, e2_ref, acc_ref, *, tt, eb, eq):
    e = pl.program_id(1)
    ib = eb // PEER_KEYS
    n_tc = tt // LANES
    pk = 16

    @pl.when(e == 0)
    def _():
        q = _dot(xn_ref[...], wq_ref[...]).astype(BF16)
        for h in range(PEER_HEADS):
            for p, ref in ((0, s1_ref), (1, s2_ref)):
                c0 = (2 * h + p) * PEER_KEYS
                s = _dot_nt(keys_ref[2 * h + p], q[:, c0:c0 + PEER_KEYS])
                for tc in range(n_tc):
                    ref[h * n_tc + tc] = s[:, tc * LANES:(tc + 1) * LANES]

        def stats(k, carry):
            s1 = s1_ref[k]
            s2 = s2_ref[k]
            a = _sorted_top16(s1)
            b = _sorted_top16(s2)
            rank2 = _ranks(s2, b)
            cand = _candidate_sums(a, b)
            tau = _top_values(cand, PEER_TOPK)[-1]
            m = a[0] + b[0]
            z = jnp.sum(jnp.where(cand >= tau, jnp.exp(cand - m), 0.0), axis=0, keepdims=True)
            count = jnp.zeros_like(s1)
            for l in range(PEER_TOPK):
                count = jnp.where(s1 + b[l] >= tau, l + 1.0, count)
            n_ref[k] = _pair_of_bf16(count)
            c1_ref[k] = _pair_of_bf16(jnp.exp(s1 - a[0]))
            r2_ref[k] = rank2.astype(BF16)
            e2_ref[k] = (jnp.exp(s2 - b[0]) / z).astype(BF16)
            return carry

        lax.fori_loop(0, PEER_HEADS * n_tc, stats, 0)
        acc_ref[...] = jnp.zeros_like(acc_ref)

    xn = xn_ref[...]

    def pre_activations(r0):
        return _dot_nt(u_ref[r0:r0 + eq, :], xn)

    hq_next = pre_activations(0)
    at_prev = None
    for r0 in range(0, eb, eq):
        hq = hq_next
        if r0 + eq < eb:
            hq_next = pre_activations(r0 + eq)
        if at_prev is not None:
            acc_ref[...] += _dot(vt_ref[0, :, r0 - eq:r0], at_prev)
        n_ii = eq // PEER_KEYS
        n_jv = PEER_KEYS // pk
        zero = jnp.zeros((pk, LANES), BF16)
        blocks = [[None] * n_tc for _ in range(n_ii)]
        for tc in range(n_tc):
            w = [[None] * n_jv for _ in range(n_ii)]
            for h in range(PEER_HEADS):
                k = h * n_tc + tc
                rank2 = [r2_ref[k, jv * pk:(jv + 1) * pk, :] for jv in range(n_jv)]
                e2 = [e2_ref[k, jv * pk:(jv + 1) * pk, :] for jv in range(n_jv)]
                for ii in range(n_ii):
                    s = r0 // PEER_KEYS + ii
                    rows8 = pl.ds(pl.multiple_of(e * ib + (s // 8) * 8, 8), 8)
                    count = _packed_row(n_ref[k, rows8, :], s % 8)
                    c1 = _packed_row(c1_ref[k, rows8, :], s % 8)
                    for jv in range(n_jv):
                        g = jnp.where(rank2[jv] <= count, e2[jv], zero) * c1
                        w[ii][jv] = g if h == 0 else w[ii][jv] + g
            for ii in range(n_ii):
                hh = hq[ii * PEER_KEYS:(ii + 1) * PEER_KEYS, tc * LANES:(tc + 1) * LANES]
                blocks[ii][tc] = _gelu(hh).astype(BF16) * jnp.concatenate(w[ii], axis=0)
        at_prev = jnp.concatenate([jnp.concatenate(row, axis=1) for row in blocks],
                                  axis=0)
    acc_ref[...] += _dot(vt_ref[0, :, eb - eq:eb], at_prev)

    @pl.when(e == pl.num_programs(1) - 1)
    def _():
        y = x2_ref[...] + acc_ref[...].T
        ms = jnp.mean(y * y, axis=-1, keepdims=True)
        o_ref[...] = y * lax.rsqrt(ms + EPS) * fw_ref[...]


def _peer(xn, x2, wq, keys, u, vt, final_w, tt, eb, eq):
    n = xn.shape[0]
    n_exp = u.shape[0]
    tok = lambda i, e: (i, 0)
    const2 = lambda i, e: (0, 0)
    def stat(dtype):
        return pltpu.VMEM((PEER_HEADS * (tt // LANES), PEER_KEYS, LANES), dtype)

    return pl.pallas_call(
        functools.partial(_peer_body, tt=tt, eb=eb, eq=eq),
        grid=(n // tt, n_exp // eb),
        in_specs=[
            pl.BlockSpec((tt, D_MODEL), tok),
            pl.BlockSpec((tt, D_MODEL), tok),
            pl.BlockSpec(wq.shape, const2),
            pl.BlockSpec(keys.shape, lambda i, e: (0, 0, 0)),
            pl.BlockSpec((eb, D_MODEL), lambda i, e: (e, 0)),
            pl.BlockSpec((1, D_MODEL, eb), lambda i, e: (e, 0, 0)),
            pl.BlockSpec((1, D_MODEL), const2),
        ],
        out_specs=pl.BlockSpec((tt, D_MODEL), tok),
        out_shape=jax.ShapeDtypeStruct((n, D_MODEL), F32),
        scratch_shapes=[stat(F32), stat(F32), stat(jnp.uint32), stat(jnp.uint32),
                        stat(BF16), stat(BF16), pltpu.VMEM((D_MODEL, tt), F32)],
        compiler_params=pltpu.CompilerParams(
            dimension_semantics=("arbitrary", "arbitrary"),
            vmem_limit_bytes=VMEM_LIMIT_BYTES),
        name="peer",
    )(xn, x2, wq, keys, u, vt, final_w)


def _pad_lanes(v):
    return jnp.pad(v, (0, LANES - v.shape[0])).reshape(1, LANES)


def _tile(n, want):
    return want if n % want == 0 else n


def _tiles(n, seq):
    return dict(
        inproj_rows=_tile(n, 1024), inproj_cols=PROJ_MAIN // 4,
        hgrn2_rows=_tile(seq, 512), ssd_rows=_tile(seq, 256), outproj_rows=_tile(n, 512),
        peer_tokens=_tile(n, 512), peer_experts=2048, peer_sub_experts=1024)


def kernel(x, norm_mix_w, w_in, hg_lb_logits, hg_norm_w, ssd_conv_w, ssd_conv_b,
           ssd_dt_bias, ssd_A_log, ssd_D, ssd_norm_w, w_out, norm_ffn_w,
           peer_w_q, peer_keys, peer_u, peer_v, final_norm_w):
    batch, seq, d = x.shape
    n = batch * seq
    x2d = x.reshape(n, d)
    width = SSD_GROUPS * SSD_GROUP_WIDTH

    w_main = w_in[0][:, :PROJ_MAIN].astype(BF16)
    w_dt = jnp.pad(w_in[0][:, PROJ_MAIN:], ((0, 0), (0, LANES - SSD_HEADS))).astype(BF16)
    conv_w = ssd_conv_w[0]
    conv_b = ssd_conv_b[0].reshape(1, -1)
    w_out1 = w_out[0][:HG_HEADS * HG_HEAD_DIM].astype(BF16)
    w_out2 = w_out[0][HG_HEADS * HG_HEAD_DIM:].astype(BF16)
    keys = peer_keys[0].reshape(PEER_HEADS * 2, PEER_KEYS, -1).astype(BF16)
    u = peer_u[0].astype(BF16)
    t = _tiles(n, seq)
    vt = peer_v[0].astype(BF16).reshape(-1, t["peer_experts"], D_MODEL).transpose(0, 2, 1)

    proj, dt_raw = _inproj(x2d, norm_mix_w[0].reshape(1, d), w_main, w_dt,
                           t["inproj_rows"], t["inproj_cols"])
    o_hg = _hgrn2(proj, hg_lb_logits, hg_norm_w[0].reshape(1, -1), batch, seq, t["hgrn2_rows"])
    o_ssd = _ssd(proj, dt_raw, conv_w[:, :width], conv_w[:, width:], conv_b[:, :width],
                 conv_b[:, width:], _pad_lanes(ssd_dt_bias[0]), _pad_lanes(ssd_A_log[0]),
                 jnp.repeat(ssd_D[0], SSD_HEAD_DIM).reshape(1, width),
                 ssd_norm_w[0].reshape(1, width), batch, seq, t["ssd_rows"])
    x2, xn = _outproj(x2d, o_hg, o_ssd, w_out1, w_out2, norm_ffn_w[0].reshape(1, d),
                      t["outproj_rows"])
    out = _peer(xn, x2, peer_w_q[0].astype(BF16), keys, u, vt, final_norm_w.reshape(1, d),
                t["peer_tokens"], t["peer_experts"], t["peer_sub_experts"])
    return out.reshape(batch, seq, d)
```

```python
import functools
import math

import jax
import jax.numpy as jnp
from jax import lax
from jax.experimental import pallas as pl
from jax.experimental.pallas import tpu as pltpu

F32 = jnp.float32
BF16 = jnp.bfloat16
EPS = 1e-6
HIGHEST = lax.Precision.HIGHEST

D_MODEL = 1024
HG_HEADS = 8
HG_HEAD_DIM = 128
CHUNK = 64
SSD_HEADS = 16
SSD_HEAD_DIM = 64
SSD_GROUPS = 2
SSD_STATE = 128
SSD_CONV = 4
SSD_GROUP_WIDTH = 512
PEER_HEADS = 8
PEER_KEYS = 128
PEER_TOPK = 16
PROJ_MAIN = 6656
LANES = 128

VMEM_LIMIT_BYTES = 56 * 1024 * 1024


def _dot(a, b):
    return jnp.dot(a, b, preferred_element_type=F32)


def _dot_nt(a, b):
    return lax.dot_general(a, b, (((1,), (1,)), ((), ())), preferred_element_type=F32)


def _dot_tn(a, b):
    return lax.dot_general(a, b, (((0,), (0,)), ((), ())), preferred_element_type=F32)


def _split3(x):
    hi = x.astype(BF16)
    r1 = x - hi.astype(F32)
    mid = r1.astype(BF16)
    lo = (r1 - mid.astype(F32)).astype(BF16)
    return hi, mid, lo


def _dot_exact(a, b, exact_side):
    if exact_side == "a":
        a16 = a.astype(BF16)
        return sum(_dot(a16, piece) for piece in _split3(b))
    b16 = b.astype(BF16)
    return sum(_dot(piece, b16) for piece in _split3(a))


def _sigmoid(x):
    return 1.0 / (1.0 + jnp.exp(-x))


def _silu(x):
    return x * _sigmoid(x)


def _softplus(x):
    return jnp.maximum(x, 0.0) + jnp.log(1.0 + jnp.exp(-jnp.abs(x)))


def _gelu(x):
    return 0.5 * x * (1.0 + lax.erf(x * (1.0 / math.sqrt(2.0))))


def _tri(n, m=None):
    m = n if m is None else m
    r = lax.broadcasted_iota(jnp.int32, (n, m), 0)
    c = lax.broadcasted_iota(jnp.int32, (n, m), 1)
    return r >= (c % n)


def _inproj_body(x_ref, nw_ref, w_ref, wdt_ref, o_ref, dt_ref, hn_ref):
    @pl.when(pl.program_id(1) == 0)
    def _():
        x = x_ref[...]
        ms = jnp.mean(x * x, axis=-1, keepdims=True)
        hn = (x * lax.rsqrt(ms + EPS) * nw_ref[...]).astype(BF16)
        hn_ref[...] = hn
        dt_ref[...] = _dot(hn, wdt_ref[...])

    o_ref[...] = _dot(hn_ref[...], w_ref[...]).astype(o_ref.dtype)


def _inproj(x2d, norm_w, w_main, w_dt, tm, tn):
    n = x2d.shape[0]
    return pl.pallas_call(
        _inproj_body,
        grid=(n // tm, PROJ_MAIN // tn),
        in_specs=[
            pl.BlockSpec((tm, D_MODEL), lambda i, j: (i, 0)),
            pl.BlockSpec((1, D_MODEL), lambda i, j: (0, 0)),
            pl.BlockSpec((D_MODEL, tn), lambda i, j: (0, j)),
            pl.BlockSpec((D_MODEL, LANES), lambda i, j: (0, 0)),
        ],
        out_specs=[
            pl.BlockSpec((tm, tn), lambda i, j: (i, j)),
            pl.BlockSpec((tm, LANES), lambda i, j: (i, 0)),
        ],
        out_shape=[
            jax.ShapeDtypeStruct((n, PROJ_MAIN), BF16),
            jax.ShapeDtypeStruct((n, LANES), F32),
        ],
        scratch_shapes=[pltpu.VMEM((tm, D_MODEL), BF16)],
        compiler_params=pltpu.CompilerParams(
            dimension_semantics=("arbitrary", "arbitrary"),
            vmem_limit_bytes=VMEM_LIMIT_BYTES),
        name="inproj",
    )(x2d, norm_w, w_main, w_dt)


def _hgrn2_body(q_ref, f_ref, i_ref, g_ref, lbl_ref, nw_ref, o_ref, st_ref, *, n_chunks):
    @pl.when(pl.program_id(1) == 0)
    def _():
        st_ref[...] = jnp.zeros_like(st_ref)

    lbl = lbl_ref[...]
    e = jnp.exp(lbl - jnp.max(lbl, axis=0, keepdims=True))
    lb = e[0:1, :] / jnp.sum(e, axis=0, keepdims=True)
    tri = _tri(CHUNK)
    tri_f = tri.astype(F32)
    nw = nw_ref[...]
    hd = HG_HEAD_DIM

    def chunk(c, carry):
        sl = pl.ds(pl.multiple_of(c * CHUNK, CHUNK), CHUNK)
        f = lb + (1.0 - lb) * _sigmoid(f_ref[sl, :].astype(F32))
        k = 1.0 - f
        g_cum = _dot_exact(tri_f, jnp.log(f), "a")
        g_last = g_cum[CHUNK - 1:CHUNK, :]
        q_dec = (q_ref[sl, :].astype(F32) * jnp.exp(g_cum)).astype(BF16)
        k_dec = (k * jnp.exp(-g_cum)).astype(BF16)
        k_end = (k * jnp.exp(g_last - g_cum)).astype(BF16)
        decay = jnp.exp(g_last)
        v = i_ref[sl, :]
        gate = _silu(g_ref[sl, :].astype(F32))
        for h in range(HG_HEADS):
            hs = slice(h * hd, (h + 1) * hd)
            scores = jnp.where(tri, _dot_nt(q_dec[:, hs], k_dec[:, hs]), 0.0)
            st = st_ref[h]
            o = _dot(scores.astype(BF16), v[:, hs]) + _dot_nt(q_dec[:, hs], st.astype(BF16))
            st_ref[h] = st * decay[:, hs] + _dot_tn(v[:, hs], k_end[:, hs])
            ms = jnp.mean(o * o, axis=-1, keepdims=True)
            o_ref[sl, hs] = (o * lax.rsqrt(ms + EPS) * nw * gate[:, hs]).astype(o_ref.dtype)
        return carry

    lax.fori_loop(0, n_chunks, chunk, 0)


def _hgrn2(proj, lb_logits, norm_w, batch, seq, ts):
    n = proj.shape[0]
    nt = seq // ts
    hd = HG_HEAD_DIM
    width = HG_HEADS * hd

    def col(k):
        return pl.BlockSpec((ts, width), lambda b, t: (b * nt + t, k))

    return pl.pallas_call(
        functools.partial(_hgrn2_body, n_chunks=ts // CHUNK),
        grid=(batch, nt),
        in_specs=[
            col(0), col(1), col(2), col(3),
            pl.BlockSpec((2, width), lambda b, t: (0, 0)),
            pl.BlockSpec((1, hd), lambda b, t: (0, 0)),
        ],
        out_specs=pl.BlockSpec((ts, width), lambda b, t: (b * nt + t, 0)),
        out_shape=jax.ShapeDtypeStruct((n, width), BF16),
        scratch_shapes=[pltpu.VMEM((HG_HEADS, hd, hd), F32)],
        compiler_params=pltpu.CompilerParams(
            dimension_semantics=("arbitrary", "arbitrary"),
            vmem_limit_bytes=VMEM_LIMIT_BYTES),
        name="hgrn2",
    )(proj, proj, proj, proj, lb_logits, norm_w)


def _ssd_body(z_ref, xs_ref, bc_ref, dt_ref, cwx_ref, cwb_ref, cbx_ref, cbb_ref,
              dtb_ref, alog_ref, dx_ref, nw_ref, o_ref,
              xpad_ref, bpad_ref, xc_ref, bcc_ref, st_ref, *, ts):
    width = SSD_GROUPS * SSD_GROUP_WIDTH
    bcw = 2 * SSD_GROUPS * SSD_STATE

    @pl.when(pl.program_id(1) == 0)
    def _():
        st_ref[...] = jnp.zeros_like(st_ref)
        xpad_ref[0:8, :] = jnp.zeros((8, width), F32)
        bpad_ref[0:8, :] = jnp.zeros((8, bcw), F32)

    xpad_ref[8:, :] = xs_ref[...].astype(F32)
    bpad_ref[8:, :] = bc_ref[...].astype(F32)
    accx = cbx_ref[...]
    accb = cbb_ref[...]
    for j in range(SSD_CONV):
        lo = 8 - (SSD_CONV - 1) + j
        accx = accx + xpad_ref[lo:lo + ts, :] * cwx_ref[j:j + 1, :]
        accb = accb + bpad_ref[lo:lo + ts, :] * cwb_ref[j:j + 1, :]
    xc_ref[...] = _silu(accx)
    bcc_ref[...] = _silu(accb)
    xpad_ref[0:8, :] = xpad_ref[ts:ts + 8, :]
    bpad_ref[0:8, :] = bpad_ref[ts:ts + 8, :]

    a_row = -jnp.exp(alog_ref[...])
    dtb = dtb_ref[...]
    tri_f = _tri(CHUNK).astype(F32)
    tri2 = _tri(CHUNK, 2 * CHUNK)
    lane = lax.broadcasted_iota(jnp.int32, (1, LANES), 1)
    left = lane < SSD_HEAD_DIM
    hrow = lax.broadcasted_iota(jnp.int32, (LANES, width), 0)
    hcol = lax.broadcasted_iota(jnp.int32, (LANES, width), 1) // SSD_HEAD_DIM
    expand = (hrow == hcol).astype(F32)

    def chunk(c, carry):
        sl = pl.ds(pl.multiple_of(c * CHUNK, CHUNK), CHUNK)
        dt = _softplus(dt_ref[sl, :] + dtb)
        cum = _dot_exact(tri_f, dt * a_row, "a")
        cum_x = _dot_exact(cum, expand, "b")
        dt_x = _dot_exact(dt, expand, "b")
        cum2 = jnp.concatenate([cum, cum], axis=0)
        cum2_t = cum2.T
        xc = xc_ref[sl, :]
        xdt = xc * dt_x
        last_x = cum_x[CHUNK - 1:CHUNK, :]
        xd_end = (xdt * jnp.exp(last_x - cum_x)).astype(BF16)
        ecum_x = jnp.exp(cum_x)
        dec_x = jnp.exp(last_x)
        xdt_b = xdt.astype(BF16)
        ys = []
        for g in range(SSD_GROUPS):
            gs = slice(g * SSD_GROUP_WIDTH, (g + 1) * SSD_GROUP_WIDTH)
            bm = bcc_ref[sl, g * SSD_STATE:(g + 1) * SSD_STATE].astype(BF16)
            cm = bcc_ref[sl, (SSD_GROUPS + g) * SSD_STATE:
                         (SSD_GROUPS + g + 1) * SSD_STATE].astype(BF16)
            cb2 = _dot_nt(cm, jnp.concatenate([bm, bm], axis=0))
            st = st_ref[g]
            y_off = _dot(cm, st.astype(BF16)) * ecum_x[:, gs]
            st_ref[g] = st * dec_x[:, gs] + _dot_tn(bm, xd_end[:, gs])
            pieces = []
            for p in range(SSD_GROUP_WIDTH // LANES):
                h0 = g * (SSD_HEADS // SSD_GROUPS) + 2 * p
                cs = slice(g * SSD_GROUP_WIDTH + p * LANES, g * SSD_GROUP_WIDTH + (p + 1) * LANES)
                col = cum_x[:, cs]
                row = jnp.where(left, cum2_t[h0:h0 + 1, :], cum2_t[h0 + 1:h0 + 2, :])
                decay = jnp.exp(jnp.where(tri2, col - row, -jnp.inf))
                m2 = (cb2 * decay).astype(BF16)
                xp = xdt_b[:, cs]
                zero = jnp.zeros_like(xp)
                x2 = jnp.concatenate([jnp.where(left, xp, zero), jnp.where(left, zero, xp)],
                                     axis=0)
                pieces.append(_dot(m2, x2))
            ys.append(jnp.concatenate(pieces, axis=1) + y_off)
        y = jnp.concatenate(ys, axis=1) + dx_ref[...] * xc
        y = y * _silu(z_ref[sl, :].astype(F32))
        nw = nw_ref[...]
        for g in range(SSD_GROUPS):
            gs = slice(g * SSD_GROUP_WIDTH, (g + 1) * SSD_GROUP_WIDTH)
            yg = y[:, gs]
            ms = jnp.mean(yg * yg, axis=-1, keepdims=True)
            o_ref[sl, gs] = (yg * lax.rsqrt(ms + EPS) * nw[:, gs]).astype(o_ref.dtype)
        return carry

    lax.fori_loop(0, ts // CHUNK, chunk, 0, unroll=2)


def _ssd(proj, dt_raw, conv_wx, conv_wb, conv_bx, conv_bb, dt_bias, a_log, d_x, norm_w,
         batch, seq, ts):
    n = proj.shape[0]
    nt = seq // ts
    width = SSD_GROUPS * SSD_GROUP_WIDTH
    bcw = 2 * SSD_GROUPS * SSD_STATE

    def full(shape):
        return pl.BlockSpec(shape, lambda b, t: (0, 0))

    return pl.pallas_call(
        functools.partial(_ssd_body, ts=ts),
        grid=(batch, nt),
        in_specs=[
            pl.BlockSpec((ts, width), lambda b, t: (b * nt + t, 4)),
            pl.BlockSpec((ts, width), lambda b, t: (b * nt + t, 5)),
            pl.BlockSpec((ts, bcw), lambda b, t: (b * nt + t, 12)),
            pl.BlockSpec((ts, LANES), lambda b, t: (b * nt + t, 0)),
            full((SSD_CONV, width)), full((SSD_CONV, bcw)),
            full((1, width)), full((1, bcw)),
            full((1, LANES)), full((1, LANES)),
            full((1, width)), full((1, width)),
        ],
        out_specs=pl.BlockSpec((ts, width), lambda b, t: (b * nt + t, 0)),
        out_shape=jax.ShapeDtypeStruct((n, width), BF16),
        scratch_shapes=[
            pltpu.VMEM((ts + 8, width), F32),
            pltpu.VMEM((ts + 8, bcw), F32),
            pltpu.VMEM((ts, width), F32),
            pltpu.VMEM((ts, bcw), F32),
            pltpu.VMEM((SSD_GROUPS, SSD_STATE, SSD_GROUP_WIDTH), F32),
        ],
        compiler_params=pltpu.CompilerParams(
            dimension_semantics=("arbitrary", "arbitrary"),
            vmem_limit_bytes=VMEM_LIMIT_BYTES),
        name="ssd",
    )(proj, proj, proj, dt_raw, conv_wx, conv_wb, conv_bx, conv_bb, dt_bias, a_log, d_x, norm_w)


def _outproj_body(x_ref, hg_ref, ssd_ref, w1_ref, w2_ref, nw_ref, x2_ref, xn_ref):
    x2 = x_ref[...] + _dot(hg_ref[...], w1_ref[...]) + _dot(ssd_ref[...], w2_ref[...])
    x2_ref[...] = x2
    ms = jnp.mean(x2 * x2, axis=-1, keepdims=True)
    xn_ref[...] = (x2 * lax.rsqrt(ms + EPS) * nw_ref[...]).astype(BF16)


def _outproj(x2d, o_hg, o_ssd, w1, w2, norm_w, tm):
    n = x2d.shape[0]
    row = pl.BlockSpec((tm, D_MODEL), lambda i: (i, 0))
    wfull = pl.BlockSpec((D_MODEL, D_MODEL), lambda i: (0, 0))
    return pl.pallas_call(
        _outproj_body,
        grid=(n // tm,),
        in_specs=[row, row, row, wfull, wfull, pl.BlockSpec((1, D_MODEL), lambda i: (0, 0))],
        out_specs=[row, row],
        out_shape=[jax.ShapeDtypeStruct((n, D_MODEL), F32),
                   jax.ShapeDtypeStruct((n, D_MODEL), BF16)],
        compiler_params=pltpu.CompilerParams(
            dimension_semantics=("arbitrary",), vmem_limit_bytes=VMEM_LIMIT_BYTES),
        name="outproj",
    )(x2d, o_hg, o_ssd, w1, w2, norm_w)


def _top_values(s, k):
    vals = []
    for _ in range(k):
        mx = jnp.max(s, axis=0, keepdims=True)
        vals.append(mx)
        s = jnp.where(s >= mx, -jnp.inf, s)
    return vals


def _candidate_sums(a, b):
    a16 = jnp.concatenate(a, axis=0)
    b16 = jnp.concatenate(b, axis=0)
    b8 = b16[0:8]
    rank = lax.broadcasted_iota(jnp.int32, b8.shape, 0) + 1
    rows = [a[0] + b16, a[1] + b8]
    for k in range(3, 9):
        rows.append(jnp.where(rank * k <= PEER_TOPK, a[k - 1] + b8, -jnp.inf))
    rows.append(a16[8:16] + b[0])
    return jnp.concatenate(rows, axis=0)


def _odd_even_merge_sort(lo, hi):
    def merge(lo, hi, r):
        step = 2 * r
        if step < hi - lo:
            yield from merge(lo, hi, step)
            yield from merge(lo + r, hi, step)
            for i in range(lo + r, hi - r, step):
                yield (i, i + r)
        else:
            yield (lo, lo + r)

    if hi > lo:
        mid = lo + (hi - lo) // 2
        yield from _odd_even_merge_sort(lo, mid)
        yield from _odd_even_merge_sort(mid + 1, hi)
        yield from merge(lo, hi, 1)


def _sorted_top16(s):
    n = PEER_TOPK
    v = [s[r * 8:(r + 1) * 8, :] for r in range(n)]
    for a, b in _odd_even_merge_sort(0, n - 1):
        v[a], v[b] = jnp.maximum(v[a], v[b]), jnp.minimum(v[a], v[b])
    for shift in (4, 2, 1):
        other = [pltpu.roll(x, shift, axis=0) for x in v]
        v = [jnp.maximum(v[r], other[n - 1 - r]) for r in range(n)]
        d = n // 2
        while d >= 1:
            for i in range(n):
                if i & d == 0:
                    v[i], v[i + d] = jnp.maximum(v[i], v[i + d]), jnp.minimum(v[i], v[i + d])
            d //= 2
    return [x[0:1, :] for x in v]


def _ranks(s, top):
    rank = jnp.full(s.shape, 2.0 * len(top), F32)
    for l in range(len(top), 0, -1):
        rank = jnp.where(s >= top[l - 1], float(l), rank)
    return rank


def _pair_of_bf16(x):
    bits = lax.bitcast_convert_type(x.astype(BF16).astype(F32), jnp.uint32) >> 16
    return bits | (bits << 16)


def _packed_row(tile, s):
    return pltpu.bitcast(jnp.broadcast_to(tile[s:s + 1, :], (8, LANES)), BF16)


def _peer_body(xn_ref, x2_ref, wq_ref, keys_ref, u_ref, vt_ref, fw_ref, o_ref,
               s1_ref, s2_ref, n_ref, c1_ref, r2_ref, e2_ref, acc_ref, *, tt, eb, eq):
    e = pl.program_id(1)
    ib = eb // PEER_KEYS
    n_tc = tt // LANES
    pk = 16

    @pl.when(e == 0)
    def _():
        q = _dot(xn_ref[...], wq_ref[...]).astype(BF16)
        for h in range(PEER_HEADS):
            for p, ref in ((0, s1_ref), (1, s2_ref)):
                c0 = (2 * h + p) * PEER_KEYS
                s = _dot_nt(keys_ref[2 * h + p], q[:, c0:c0 + PEER_KEYS])
                for tc in range(n_tc):
                    ref[h * n_tc + tc] = s[:, tc * LANES:(tc + 1) * LANES]

        def stats(k, carry):
            s1 = s1_ref[k]
            s2 = s2_ref[k]
            a = _sorted_top16(s1)
            b = _sorted_top16(s2)
            rank2 = _ranks(s2, b)
            cand = _candidate_sums(a, b)
            tau = _top_values(cand, PEER_TOPK)[-1]
            m = a[0] + b[0]
            z = jnp.sum(jnp.where(cand >= tau, jnp.exp(cand - m), 0.0), axis=0, keepdims=True)
            count = jnp.zeros_like(s1)
            for l in range(PEER_TOPK):
                count = jnp.where(s1 + b[l] >= tau, l + 1.0, count)
            n_ref[k] = _pair_of_bf16(count)
            c1_ref[k] = _pair_of_bf16(jnp.exp(s1 - a[0]))
            r2_ref[k] = rank2.astype(BF16)
            e2_ref[k] = (jnp.exp(s2 - b[0]) / z).astype(BF16)
            return carry

        lax.fori_loop(0, PEER_HEADS * n_tc, stats, 0)
        acc_ref[...] = jnp.zeros_like(acc_ref)

    xn = xn_ref[...]

    def pre_activations(r0):
        return _dot_nt(u_ref[r0:r0 + eq, :], xn)

    hq_next = pre_activations(0)
    at_prev = None
    for r0 in range(0, eb, eq):
        hq = hq_next
        if r0 + eq < eb:
            hq_next = pre_activations(r0 + eq)
        if at_prev is not None:
            acc_ref[...] += _dot(vt_ref[0, :, r0 - eq:r0], at_prev)
        n_ii = eq // PEER_KEYS
        n_jv = PEER_KEYS // pk
        zero = jnp.zeros((pk, LANES), BF16)
        blocks = [[None] * n_tc for _ in range(n_ii)]
        for tc in range(n_tc):
            w = [[None] * n_jv for _ in range(n_ii)]
            for h in range(PEER_HEADS):
                k = h * n_tc + tc
                rank2 = [r2_ref[k, jv * pk:(jv + 1) * pk, :] for jv in range(n_jv)]
                e2 = [e2_ref[k, jv * pk:(jv + 1) * pk, :] for jv in range(n_jv)]
                for ii in range(n_ii):
                    s = r0 // PEER_KEYS + ii
                    rows8 = pl.ds(pl.multiple_of(e * ib + (s // 8) * 8, 8), 8)
                    count = _packed_row(n_ref[k, rows8, :], s % 8)
                    c1 = _packed_row(c1_ref[k, rows8, :], s % 8)
                    for jv in range(n_jv):
                        g = jnp.where(rank2[jv] <= count, e2[jv], zero) * c1
                        w[ii][jv] = g if h == 0 else w[ii][jv] + g
            for ii in range(n_ii):
                hh = hq[ii * PEER_KEYS:(ii + 1) * PEER_KEYS, tc * LANES:(tc + 1) * LANES]
                blocks[ii][tc] = _gelu(hh).astype(BF16) * jnp.concatenate(w[ii], axis=0)
        at_prev = jnp.concatenate([jnp.concatenate(row, axis=1) for row in blocks],
                                  axis=0)
    acc_ref[...] += _dot(vt_ref[0, :, eb - eq:eb], at_prev)

    @pl.when(e == pl.num_programs(1) - 1)
    def _():
        y = x2_ref[...] + acc_ref[...].T
        ms = jnp.mean(y * y, axis=-1, keepdims=True)
        o_ref[...] = y * lax.rsqrt(ms + EPS) * fw_ref[...]


def _peer(xn, x2, wq, keys, u, vt, final_w, tt, eb, eq):
    n = xn.shape[0]
    n_exp = u.shape[0]
    tok = lambda i, e: (i, 0)
    const2 = lambda i, e: (0, 0)
    def stat(dtype):
        return pltpu.VMEM((PEER_HEADS * (tt // LANES), PEER_KEYS, LANES), dtype)

    return pl.pallas_call(
        functools.partial(_peer_body, tt=tt, eb=eb, eq=eq),
        grid=(n // tt, n_exp // eb),
        in_specs=[
            pl.BlockSpec((tt, D_MODEL), tok),
            pl.BlockSpec((tt, D_MODEL), tok),
            pl.BlockSpec(wq.shape, const2),
            pl.BlockSpec(keys.shape, lambda i, e: (0, 0, 0)),
            pl.BlockSpec((eb, D_MODEL), lambda i, e: (e, 0)),
            pl.BlockSpec((1, D_MODEL, eb), lambda i, e: (e, 0, 0)),
            pl.BlockSpec((1, D_MODEL), const2),
        ],
        out_specs=pl.BlockSpec((tt, D_MODEL), tok),
        out_shape=jax.ShapeDtypeStruct((n, D_MODEL), F32),
        scratch_shapes=[stat(F32), stat(F32), stat(jnp.uint32), stat(jnp.uint32),
                        stat(BF16), stat(BF16), pltpu.VMEM((D_MODEL, tt), F32)],
        compiler_params=pltpu.CompilerParams(
            dimension_semantics=("arbitrary", "arbitrary"),
            vmem_limit_bytes=VMEM_LIMIT_BYTES),
        name="peer",
    )(xn, x2, wq, keys, u, vt, final_w)


def _pad_lanes(v):
    return jnp.pad(v, (0, LANES - v.shape[0])).reshape(1, LANES)


def _tile(n, want):
    return want if n % want == 0 else n


def _tiles(n, seq):
    return dict(
        inproj_rows=_tile(n, 2048), inproj_cols=PROJ_MAIN // 4,
        hgrn2_rows=_tile(seq, 1024), ssd_rows=_tile(seq, 512), outproj_rows=_tile(n, 1024),
        peer_tokens=_tile(n, 512), peer_experts=2048, peer_sub_experts=1024)


def kernel(x, norm_mix_w, w_in, hg_lb_logits, hg_norm_w, ssd_conv_w, ssd_conv_b,
           ssd_dt_bias, ssd_A_log, ssd_D, ssd_norm_w, w_out, norm_ffn_w,
           peer_w_q, peer_keys, peer_u, peer_v, final_norm_w):
    batch, seq, d = x.shape
    n = batch * seq
    x2d = x.reshape(n, d)
    width = SSD_GROUPS * SSD_GROUP_WIDTH

    w_main = w_in[0][:, :PROJ_MAIN].astype(BF16)
    w_dt = jnp.pad(w_in[0][:, PROJ_MAIN:], ((0, 0), (0, LANES - SSD_HEADS))).astype(BF16)
    conv_w = ssd_conv_w[0]
    conv_b = ssd_conv_b[0].reshape(1, -1)
    w_out1 = w_out[0][:HG_HEADS * HG_HEAD_DIM].astype(BF16)
    w_out2 = w_out[0][HG_HEADS * HG_HEAD_DIM:].astype(BF16)
    keys = peer_keys[0].reshape(PEER_HEADS * 2, PEER_KEYS, -1).astype(BF16)
    u = peer_u[0].astype(BF16)
    t = _tiles(n, seq)
    vt = peer_v[0].astype(BF16).reshape(-1, t["peer_experts"], D_MODEL).transpose(0, 2, 1)

    proj, dt_raw = _inproj(x2d, norm_mix_w[0].reshape(1, d), w_main, w_dt,
                           t["inproj_rows"], t["inproj_cols"])
    o_hg = _hgrn2(proj, hg_lb_logits, hg_norm_w[0].reshape(1, -1), batch, seq, t["hgrn2_rows"])
    o_ssd = _ssd(proj, dt_raw, conv_w[:, :width], conv_w[:, width:], conv_b[:, :width],
                 conv_b[:, width:], _pad_lanes(ssd_dt_bias[0]), _pad_lanes(ssd_A_log[0]),
                 jnp.repeat(ssd_D[0], SSD_HEAD_DIM).reshape(1, width),
                 ssd_norm_w[0].reshape(1, width), batch, seq, t["ssd_rows"])
    x2, xn = _outproj(x2d, o_hg, o_ssd, w_out1, w_out2, norm_ffn_w[0].reshape(1, d),
                      t["outproj_rows"])
    out = _peer(xn, x2, peer_w_q[0].astype(BF16), keys, u, vt, final_norm_w.reshape(1, d),
                t["peer_tokens"], t["peer_experts"], t["peer_sub_experts"])
    return out.reshape(batch, seq, d)
```
